```python
import jax
import jax.numpy as jnp
from jax import lax
import numpy as np

D_MODEL = 1024
BATCH = 8
SEQ = 4096
DEPTH = 2

HEAD_DIM = 64
NSA_HEADS = 6
NSA_KV_GROUPS = 2
NSA_HPG = NSA_HEADS // NSA_KV_GROUPS
SB_HEADS = 5
MLA_HEADS = 5
NSA_D = NSA_HEADS * HEAD_DIM
NSA_KV = NSA_KV_GROUPS * HEAD_DIM
SB_D = SB_HEADS * HEAD_DIM
MLA_NOPE = 64
MLA_ROPE = 32
MLA_V = 64
MLA_Q_RANK = 192
MLA_KV_RANK = 128
MLA_D = MLA_HEADS * MLA_V
D_MIX = NSA_D + SB_D + MLA_D
CMP_LEN = 32
CMP_STRIDE = 16
CMP_HIDDEN = 128
SEL_BLOCK = 64
SEL_TOPK = 16
WINDOW = 512
Q_BLOCK = 128
ROPE_THETA = 10000.0
EPS = 1e-6
NEG_INF = -1e30
SEL_FORCE = 1e9

IN_WIDTHS = (
    NSA_D,
    NSA_KV, NSA_KV, NSA_KV, NSA_KV, NSA_KV, NSA_KV,
    NSA_HEADS * 3,
    NSA_D,
    SB_D, SB_D, SB_D, SB_D,
    MLA_Q_RANK, MLA_KV_RANK, MLA_ROPE,
    MLA_D,
)
D_IN = sum(IN_WIDTHS)
IN_SPLITS = tuple(int(v) for v in np.cumsum(IN_WIDTHS)[:-1])

kernel_name = 'hybrid_nsa_stickbreak_mla_block'


def rmsnorm(x, g):
    xf = x.astype(jnp.float32)
    y = xf * lax.rsqrt(jnp.mean(xf * xf, axis=-1, keepdims=True) + EPS)
    return (y * g.astype(jnp.float32)).astype(x.dtype)


def rope(x, pos):
    half = x.shape[-1] // 2
    inv = ROPE_THETA ** (-jnp.arange(half, dtype=jnp.float32) / half)
    ang = pos.astype(jnp.float32)[..., None] * inv
    ang = ang.reshape((ang.shape[0],) + (1,) * (x.ndim - 3) + ang.shape[1:])
    cos, sin = jnp.cos(ang), jnp.sin(ang)
    xf = x.astype(jnp.float32)
    x1, x2 = xf[..., :half], xf[..., half:]
    return jnp.concatenate([x1 * cos - x2 * sin, x2 * cos + x1 * sin], axis=-1).astype(x.dtype)


def masked_softmax(s, mask):
    p = jax.nn.softmax(jnp.where(mask, s, NEG_INF), axis=-1)
    return jnp.where(mask, p, 0.0)


def nsa_attention(q, k_cmp, v_cmp, k_slc, v_slc, k_win, v_win, gates,
                  pos_k, pos_v, ck_w1, ck_w2, cv_w1, cv_w2):
    B, G, HPG, S, D = q.shape
    scale = D ** -0.5
    n_cmp = (S - CMP_LEN) // CMP_STRIDE + 1
    cmp_idx = np.arange(n_cmp)[:, None] * CMP_STRIDE + np.arange(CMP_LEN)[None, :]
    cmp_end = jnp.asarray(cmp_idx[:, -1])

    def compress(tok, pos_emb, w1, w2):
        blk = (tok[:, :, cmp_idx] + pos_emb).reshape(B, G, n_cmp, CMP_LEN * D)
        return jax.nn.silu(blk @ w1) @ w2

    kc = compress(k_cmp, pos_k, ck_w1, ck_w2)
    vc = compress(v_cmp, pos_v, cv_w1, cv_w2)

    n_sel = S // SEL_BLOCK
    top = min(SEL_TOPK, n_sel)
    c_start = cmp_idx[:, 0]
    s_start = np.arange(n_sel) * SEL_BLOCK
    overlap = jnp.asarray((c_start[:, None] < s_start[None, :] + SEL_BLOCK)
                          & (c_start[:, None] + CMP_LEN > s_start[None, :]), dtype=jnp.float32)
    ks_blk = k_slc.reshape(B, G, n_sel, SEL_BLOCK, D)
    vs_blk = v_slc.reshape(B, G, n_sel, SEL_BLOCK, D)
    kw_pad = jnp.pad(k_win, ((0, 0), (0, 0), (WINDOW, 0), (0, 0)))
    vw_pad = jnp.pad(v_win, ((0, 0), (0, 0), (WINDOW, 0), (0, 0)))
    b_ix = jnp.arange(B)[:, None, None, None]
    g_ix = jnp.arange(G)[None, :, None, None]
    sel_j = jnp.arange(n_sel)

    def block(i):
        q0 = i * Q_BLOCK
        t = q0 + jnp.arange(Q_BLOCK)
        qb = lax.dynamic_slice_in_dim(q, q0, Q_BLOCK, axis=3)
        gb = lax.dynamic_slice_in_dim(gates, q0, Q_BLOCK, axis=3)
        s_c = jnp.einsum('bghtd,bgnd->bghtn', qb, kc).astype(jnp.float32) * scale
        p_c = masked_softmax(s_c, cmp_end[None, :] <= t[:, None])
        o_c = jnp.einsum('bghtn,bgnd->bghtd', p_c.astype(vc.dtype), vc)
        imp = jnp.einsum('bghtn,nj->bgtj', p_c, overlap)
        cur = (t // SEL_BLOCK)[:, None]
        forced = (sel_j == 0) | (sel_j == cur) | (sel_j == cur - 1)
        valid = sel_j * SEL_BLOCK <= t[:, None]
        imp = jnp.where(valid, jnp.where(forced, SEL_FORCE, imp), -SEL_FORCE)
        _, top_idx = lax.top_k(imp, top)
        k_g = ks_blk[b_ix, g_ix, top_idx].reshape(B, G, Q_BLOCK, top * SEL_BLOCK, D)
        v_g = vs_blk[b_ix, g_ix, top_idx].reshape(B, G, Q_BLOCK, top * SEL_BLOCK, D)
        tok = (top_idx[..., None] * SEL_BLOCK + jnp.arange(SEL_BLOCK)).reshape(B, G, Q_BLOCK, top * SEL_BLOCK)
        m_s = (tok <= t[:, None])[:, :, None]
        s_s = jnp.einsum('bghtd,bgtkd->bghtk', qb, k_g).astype(jnp.float32) * scale
        p_s = masked_softmax(s_s, m_s)
        o_s = jnp.einsum('bghtk,bgtkd->bghtd', p_s.astype(v_g.dtype), v_g)
        kwb = lax.dynamic_slice_in_dim(kw_pad, q0, Q_BLOCK + WINDOW, axis=2)
        vwb = lax.dynamic_slice_in_dim(vw_pad, q0, Q_BLOCK + WINDOW, axis=2)
        s_pos = q0 - WINDOW + jnp.arange(Q_BLOCK + WINDOW)
        m_w = (s_pos[None, :] <= t[:, None]) & (s_pos[None, :] > t[:, None] - WINDOW) & (s_pos[None, :] >= 0)
        s_w = jnp.einsum('bghtd,bgsd->bghts', qb, kwb).astype(jnp.float32) * scale
        p_w = masked_softmax(s_w, m_w)
        o_w = jnp.einsum('bghts,bgsd->bghtd', p_w.astype(vwb.dtype), vwb)
        return gb[..., 0:1] * o_c + gb[..., 1:2] * o_s + gb[..., 2:3] * o_w

    out = lax.map(block, jnp.arange(S // Q_BLOCK))
    return out.transpose(1, 0, 4, 2, 3, 5).reshape(B, S, G * HPG * D)


def stick_breaking_attention(q, k, v):
    B, H, S, D = q.shape
    scale = D ** -0.5
    s_pos = jnp.arange(S)

    def block(i):
        q0 = i * Q_BLOCK
        t = q0 + jnp.arange(Q_BLOCK)
        qb = lax.dynamic_slice_in_dim(q, q0, Q_BLOCK, axis=2)
        z = jnp.einsum('bhtd,bhsd->bhts', qb, k).astype(jnp.float32) * scale
        mask = s_pos[None, :] < t[:, None]
        log_1mb = jnp.where(mask, jax.nn.log_sigmoid(-z), 0.0)
        rev = lax.cumsum(log_1mb, axis=3, reverse=True)
        between = jnp.concatenate([rev[..., 1:], jnp.zeros_like(rev[..., :1])], axis=-1)
        a = jnp.where(mask, jnp.exp(jax.nn.log_sigmoid(z) + between), 0.0)
        return jnp.einsum('bhts,bhsd->bhtd', a.astype(v.dtype), v)

    out = lax.map(block, jnp.arange(S // Q_BLOCK))
    return out.transpose(1, 0, 3, 2, 4).reshape(B, S, H * D)


def causal_attention(q, k, v):
    B, H, S, Dk = q.shape
    scale = Dk ** -0.5
    s_pos = jnp.arange(S)

    def block(i):
        q0 = i * Q_BLOCK
        t = q0 + jnp.arange(Q_BLOCK)
        qb = lax.dynamic_slice_in_dim(q, q0, Q_BLOCK, axis=2)
        s = jnp.einsum('bhtd,bhsd->bhts', qb, k).astype(jnp.float32) * scale
        p = masked_softmax(s, s_pos[None, :] <= t[:, None])
        return jnp.einsum('bhts,bhsd->bhtd', p.astype(v.dtype), v)

    out = lax.map(block, jnp.arange(S // Q_BLOCK))
    return out.transpose(1, 0, 3, 2, 4).reshape(B, S, H * v.shape[-1])


def hybrid_mixer(h, positions, w_in, pos_k, pos_v, ck_w1, ck_w2, cv_w1, cv_w2,
                 q_norm, w_uq, kv_norm, w_ukv, w_out):
    B, S, _ = h.shape
    (n_q, n_kc, n_vc, n_ks, n_vs, n_kw, n_vw, n_gate, n_z,
     s_q, s_k, s_v, s_z, m_cq, m_ckv, m_kr, m_z) = jnp.split(h @ w_in, IN_SPLITS, axis=-1)

    def heads(t, n):
        return t.reshape(B, S, n, -1).transpose(0, 2, 1, 3)

    q_a = rope(n_q.reshape(B, S, NSA_KV_GROUPS, NSA_HPG, HEAD_DIM).transpose(0, 2, 3, 1, 4), positions)
    gates = jax.nn.sigmoid(n_gate.reshape(B, S, NSA_KV_GROUPS, NSA_HPG, 3).transpose(0, 2, 3, 1, 4))
    o_a = nsa_attention(q_a,
                        rope(heads(n_kc, NSA_KV_GROUPS), positions), heads(n_vc, NSA_KV_GROUPS),
                        rope(heads(n_ks, NSA_KV_GROUPS), positions), heads(n_vs, NSA_KV_GROUPS),
                        rope(heads(n_kw, NSA_KV_GROUPS), positions), heads(n_vw, NSA_KV_GROUPS),
                        gates, pos_k, pos_v, ck_w1, ck_w2, cv_w1, cv_w2) * jax.nn.silu(n_z)

    o_b = stick_breaking_attention(heads(s_q, SB_HEADS), heads(s_k, SB_HEADS),
                                   heads(s_v, SB_HEADS)) * jax.nn.silu(s_z)

    qh = heads(rmsnorm(m_cq, q_norm) @ w_uq, MLA_HEADS)
    kvh = heads(rmsnorm(m_ckv, kv_norm) @ w_ukv, MLA_HEADS)
    k_pe = rope(m_kr[:, None], positions)
    q_c = jnp.concatenate([qh[..., :MLA_NOPE], rope(qh[..., MLA_NOPE:], positions)], axis=-1)
    k_c = jnp.concatenate([kvh[..., :MLA_NOPE], jnp.broadcast_to(k_pe, (B, MLA_HEADS, S, MLA_ROPE))], axis=-1)
    o_c = causal_attention(q_c, k_c, kvh[..., MLA_NOPE:]) * jax.nn.silu(m_z)

    return jnp.concatenate([o_a, o_b, o_c], axis=-1) @ w_out


def setup_inputs(seed: int = 0) -> dict:
    key = jax.random.key(seed)
    ks = jax.random.split(key, 24)

    def nrm(k, shape, scale):
        return jax.random.normal(k, shape, jnp.float32) * scale

    x = nrm(ks[0], (BATCH, SEQ, D_MODEL), 1.0)
    c = nrm(ks[1], (BATCH, D_MODEL), 1.0)
    start = jax.random.randint(ks[2], (BATCH, 1), 0, 2048, dtype=jnp.int32)
    positions = (start + jnp.arange(SEQ, dtype=jnp.int32)[None, :]).astype(jnp.int32)
    return {
        'x': x,
        'c': c,
        'positions': positions,
        'ada_w': nrm(ks[3], (DEPTH, D_MODEL, 3 * D_MODEL), 0.5 * D_MODEL ** -0.5),
        'ada_b': nrm(ks[4], (DEPTH, 3 * D_MODEL), 0.01),
        'norm_g': 1.0 + nrm(ks[5], (DEPTH, D_MODEL), 0.01),
        'w_in': nrm(ks[6], (DEPTH, D_MODEL, D_IN), D_MODEL ** -0.5),
        'nsa_pos_k': nrm(ks[7], (DEPTH, CMP_LEN, HEAD_DIM), 0.1),
        'nsa_pos_v': nrm(ks[8], (DEPTH, CMP_LEN, HEAD_DIM), 0.1),
        'nsa_ck_w1': nrm(ks[9], (DEPTH, CMP_LEN * HEAD_DIM, CMP_HIDDEN), (CMP_LEN * HEAD_DIM) ** -0.5),
        'nsa_ck_w2': nrm(ks[10], (DEPTH, CMP_HIDDEN, HEAD_DIM), CMP_HIDDEN ** -0.5),
        'nsa_cv_w1': nrm(ks[11], (DEPTH, CMP_LEN * HEAD_DIM, CMP_HIDDEN), (CMP_LEN * HEAD_DIM) ** -0.5),
        'nsa_cv_w2': nrm(ks[12], (DEPTH, CMP_HIDDEN, HEAD_DIM), CMP_HIDDEN ** -0.5),
        'mla_q_norm': 1.0 + nrm(ks[13], (DEPTH, MLA_Q_RANK), 0.01),
        'mla_w_uq': nrm(ks[14], (DEPTH, MLA_Q_RANK, MLA_HEADS * (MLA_NOPE + MLA_ROPE)), MLA_Q_RANK ** -0.5),
        'mla_kv_norm': 1.0 + nrm(ks[15], (DEPTH, MLA_KV_RANK), 0.01),
        'mla_w_ukv': nrm(ks[16], (DEPTH, MLA_KV_RANK, MLA_HEADS * (MLA_NOPE + MLA_V)), MLA_KV_RANK ** -0.5),
        'w_out': nrm(ks[17], (DEPTH, D_MIX, D_MODEL), D_MIX ** -0.5),
        'final_norm': 1.0 + nrm(ks[18], (D_MODEL,), 0.01),
    }


def reference(x, c, positions, ada_w, ada_b, norm_g, w_in, nsa_pos_k, nsa_pos_v,
              nsa_ck_w1, nsa_ck_w2, nsa_cv_w1, nsa_cv_w2, mla_q_norm, mla_w_uq,
              mla_kv_norm, mla_w_ukv, w_out, final_norm):
    for l in range(DEPTH):
        mod = jax.nn.silu(c) @ ada_w[l] + ada_b[l]
        shift, scale, gate = jnp.split(mod, 3, axis=-1)
        h = rmsnorm(x, norm_g[l]) * (1.0 + scale[:, None, :]) + shift[:, None, :]
        y = hybrid_mixer(h, positions, w_in[l], nsa_pos_k[l], nsa_pos_v[l],
                         nsa_ck_w1[l], nsa_ck_w2[l], nsa_cv_w1[l], nsa_cv_w2[l],
                         mla_q_norm[l], mla_w_uq[l], mla_kv_norm[l], mla_w_ukv[l], w_out[l])
        x = x + gate[:, None, :] * y
    return rmsnorm(x, final_norm)
```

```python
import functools

import numpy as np
import jax
import jax.numpy as jnp
from jax import lax
from jax.experimental import pallas as pl
from jax.experimental.pallas import tpu as pltpu

F32 = jnp.float32
BF16 = jnp.bfloat16

LANES = 128
HEAD_DIM = 64
NSA_HEADS = 6
NSA_GROUPS = 2
NSA_HPG = NSA_HEADS // NSA_GROUPS
SB_HEADS = 5
MLA_HEADS = 5
MLA_NOPE = 64
MLA_ROPE = 32
MLA_V = 64
MLA_Q_RANK = 192
MLA_KV_RANK = 128
CMP_LEN = 32
CMP_STRIDE = 16
CMP_HIDDEN = 128
SEL_BLOCK = 64
SEL_TOPK = 16
WINDOW = 512
ROPE_THETA = 10000.0
EPS = 1e-6
SEL_FORCE = 1e9
M_FLOOR = -1e30

NSA_D = NSA_HEADS * HEAD_DIM
PAIR_SLABS = 3
PAIR_D = PAIR_SLABS * LANES
Q_RANK_PAD = 256

C_NQ = 0
C_NKV = C_NQ + NSA_D
C_CMP = C_NKV + 4 * LANES
C_GATE = C_CMP + 2 * LANES
C_Z = C_GATE + LANES
C_SB = C_Z + 3 * PAIR_D
C_CQ = C_SB + 3 * PAIR_D
C_CKV = C_CQ + Q_RANK_PAD
C_KR = C_CKV + LANES
D_IN_PAD = C_KR + LANES

VMEM_LIMIT = 48 * 1024 * 1024


def _cparams(sem):
    return pltpu.CompilerParams(dimension_semantics=sem, vmem_limit_bytes=VMEM_LIMIT)


def _dot(a, b):
    return jnp.dot(a, b, preferred_element_type=F32)


def _dot_nt(a, b):
    return lax.dot_general(a, b, (((1,), (1,)), ((), ())), preferred_element_type=F32)


def _iota(shape, dim):
    return lax.broadcasted_iota(jnp.int32, shape, dim)


def _rope(x, cos, sin_signed, width):
    half = width // 2
    lane = _iota(x.shape, 1)
    first = (lane % width) < half
    rot = jnp.where(first, pltpu.roll(x, LANES - half, 1), pltpu.roll(x, half, 1))
    return x * cos + rot * sin_signed


def _in_cols():
    widths = (NSA_D, 128, 128, 128, 128, 128, 128, NSA_HEADS * 3, NSA_D,
              320, 320, 320, 320, MLA_Q_RANK, MLA_KV_RANK, MLA_ROPE, 320)
    off = np.concatenate([[0], np.cumsum(widths)])
    (o_nq, o_kc, o_vc, o_ks, o_vs, o_kw, o_vw, o_gate, o_nz,
     o_sq, o_sk, o_sv, o_sz, o_cq, o_ckv, o_kr, o_mz) = [int(v) for v in off[:-1]]
    cols = -np.ones((D_IN_PAD,), np.int64)

    def put(dst, src, n):
        cols[dst:dst + n] = src + np.arange(n)

    def put_nsa(dst, src):
        for i in range(NSA_HPG):
            for g in range(NSA_GROUPS):
                put(dst + i * LANES + g * HEAD_DIM, src + (g * NSA_HPG + i) * HEAD_DIM, HEAD_DIM)

    put_nsa(C_NQ, o_nq)
    put(C_NKV, o_ks, 128)
    put(C_NKV + 128, o_vs, 128)
    put(C_NKV + 256, o_kw, 128)
    put(C_NKV + 384, o_vw, 128)
    put(C_CMP, o_kc, 128)
    put(C_CMP + 128, o_vc, 128)
    put(C_GATE, o_gate, NSA_HEADS * 3)
    put_nsa(C_Z, o_nz)
    put(C_Z + PAIR_D, o_sz, 320)
    put(C_Z + 2 * PAIR_D, o_mz, 320)
    put(C_SB, o_sq, 320)
    put(C_SB + PAIR_D, o_sk, 320)
    put(C_SB + 2 * PAIR_D, o_sv, 320)
    put(C_CQ, o_cq, MLA_Q_RANK)
    put(C_CKV, o_ckv, MLA_KV_RANK)
    put(C_KR, o_kr, MLA_ROPE)
    put(C_KR + MLA_ROPE, o_kr, MLA_ROPE)
    return cols


def _take_cols(w, cols):
    cols = np.asarray(cols)
    g = jnp.take(w, jnp.asarray(np.maximum(cols, 0)), axis=1)
    return jnp.where(jnp.asarray(cols >= 0)[None, :], g, 0.0)


def _uq_cols():
    cols = -np.ones((2 * PAIR_D,), np.int64)
    for h in range(MLA_HEADS):
        p, a = divmod(h, 2)
        base = h * (MLA_NOPE + MLA_ROPE)
        cols[p * LANES + a * MLA_NOPE: p * LANES + (a + 1) * MLA_NOPE] = base + np.arange(MLA_NOPE)
        d = PAIR_D + p * LANES + a * MLA_ROPE
        cols[d: d + MLA_ROPE] = base + MLA_NOPE + np.arange(MLA_ROPE)
    return cols


def _ukv_cols():
    cols = -np.ones((2 * PAIR_D,), np.int64)
    for h in range(MLA_HEADS):
        base = h * (MLA_NOPE + MLA_V)
        cols[h * MLA_NOPE: (h + 1) * MLA_NOPE] = base + np.arange(MLA_NOPE)
        cols[PAIR_D + h * MLA_V: PAIR_D + (h + 1) * MLA_V] = base + MLA_NOPE + np.arange(MLA_V)
    return cols


def _out_rows():
    rows = -np.ones((3 * PAIR_D,), np.int64)
    for i in range(NSA_HPG):
        for g in range(NSA_GROUPS):
            d = i * LANES + g * HEAD_DIM
            rows[d:d + HEAD_DIM] = (g * NSA_HPG + i) * HEAD_DIM + np.arange(HEAD_DIM)
    rows[PAIR_D:PAIR_D + 320] = NSA_D + np.arange(320)
    rows[2 * PAIR_D:2 * PAIR_D + 320] = NSA_D + 320 + np.arange(320)
    return rows


def _cmp_weights(w1, w2):
    half = CMP_LEN // 2
    w1r = w1.reshape(CMP_LEN, HEAD_DIM, CMP_HIDDEN)
    z = jnp.zeros((half, HEAD_DIM, CMP_HIDDEN), w1.dtype)

    def build(part):
        g0 = jnp.concatenate([part, z], axis=1)
        g1 = jnp.concatenate([z, part], axis=1)
        return jnp.concatenate([g0.reshape(half * LANES, CMP_HIDDEN),
                                g1.reshape(half * LANES, CMP_HIDDEN)], axis=1)

    zz = jnp.zeros_like(w2)
    w2bd = jnp.concatenate([jnp.concatenate([w2, zz], axis=1),
                            jnp.concatenate([zz, w2], axis=1)], axis=0)
    return build(w1r[:half]).astype(BF16), build(w1r[half:]).astype(BF16), w2bd.astype(BF16)


def _cmp_pos(pos):
    half = CMP_LEN // 2
    tiled = jnp.concatenate([pos, pos], axis=1)
    return jnp.stack([tiled[:half].reshape(half * LANES), tiled[half:].reshape(half * LANES)])


def _mod_kernel(c_ref, w_ref, b_ref, o_ref):
    a = jax.nn.silu(c_ref[...])
    o_ref[0] = _dot(a, w_ref[0]) + b_ref[0]


def _adaln_mod(c, ada_w, ada_b):
    depth, d, n = ada_w.shape
    b = c.shape[0]
    tn = 1024
    return pl.pallas_call(
        _mod_kernel,
        grid=(depth, n // tn),
        in_specs=[pl.BlockSpec((b, d), lambda l, j: (0, 0)),
                  pl.BlockSpec((1, d, tn), lambda l, j: (l, 0, j)),
                  pl.BlockSpec((1, 1, tn), lambda l, j: (l, 0, j))],
        out_specs=pl.BlockSpec((1, b, tn), lambda l, j: (l, 0, j)),
        out_shape=jax.ShapeDtypeStruct((depth, b, n), F32),
        compiler_params=_cparams(("arbitrary", "arbitrary")),
        name="adaln_mod",
    )(c, ada_w, ada_b.reshape(depth, 1, n))


def _rope_kernel(pos_ref, inv_ref, c64_ref, s64_ref, c32_ref, s32_ref):
    pos = pos_ref[0].astype(F32)
    lane = _iota((1, LANES), 1)
    a64 = pos * inv_ref[0:1, :]
    a32 = pos * inv_ref[1:2, :]
    c64_ref[0] = jnp.cos(a64)
    s64 = jnp.sin(a64)
    s64_ref[0] = jnp.where((lane % HEAD_DIM) < HEAD_DIM // 2, -s64, s64)
    c32_ref[0] = jnp.cos(a32)
    s32 = jnp.sin(a32)
    s32_ref[0] = jnp.where((lane % MLA_ROPE) < MLA_ROPE // 2, -s32, s32)


def _rope_tables(positions):
    b, s = positions.shape
    ts = 512

    def inv(half):
        v = ROPE_THETA ** (-jnp.arange(half, dtype=F32) / half)
        return jnp.tile(v, LANES // half)

    inv_tab = jnp.zeros((8, LANES), F32).at[0].set(inv(HEAD_DIM // 2)).at[1].set(inv(MLA_ROPE // 2))
    tab = jax.ShapeDtypeStruct((b, s, LANES), F32)
    spec = pl.BlockSpec((1, ts, LANES), lambda i, j: (i, j, 0))
    return pl.pallas_call(
        _rope_kernel,
        grid=(b, s // ts),
        in_specs=[pl.BlockSpec((1, ts, 1), lambda i, j: (i, j, 0)),
                  pl.BlockSpec((8, LANES), lambda i, j: (0, 0))],
        out_specs=[spec] * 4,
        out_shape=[tab] * 4,
        compiler_params=_cparams(("arbitrary", "arbitrary")),
        name="rope_tables",
    )(positions.reshape(b, s, 1), inv_tab)


def _inproj_kernel(x_ref, mod_ref, g_ref, w_ref, qn_ref, wuq_ref, kvn_ref, wukv_ref,
                   c64_ref, s64_ref, c32_ref, s32_ref,
                   nq_ref, nkv_ref, cmp_ref, gate_ref, z_ref, sb_ref, mla_ref):
    x = x_ref[0]
    ms = jnp.mean(x * x, axis=-1, keepdims=True)
    y = x * lax.rsqrt(ms + EPS) * g_ref[...]
    h = y * (1.0 + mod_ref[0, 1:2, :]) + mod_ref[0, 0:1, :]
    hb = h.astype(BF16)

    def proj(c0, n):
        return _dot(hb, w_ref[:, c0:c0 + n])

    c64, s64 = c64_ref[0], s64_ref[0]
    c32, s32 = c32_ref[0], s32_ref[0]
    q_scale = HEAD_DIM ** -0.5

    for i in range(NSA_HPG):
        q = _rope(proj(C_NQ + i * LANES, LANES), c64, s64, HEAD_DIM)
        nq_ref[0, :, i * LANES:(i + 1) * LANES] = (q * q_scale).astype(BF16)
    for j in range(4):
        t = proj(C_NKV + j * LANES, LANES)
        if j % 2 == 0:
            t = _rope(t, c64, s64, HEAD_DIM)
        nkv_ref[0, :, j * LANES:(j + 1) * LANES] = t.astype(BF16)
    cmp_ref[0, :, 0:LANES] = _rope(proj(C_CMP, LANES), c64, s64, HEAD_DIM)
    cmp_ref[0, :, LANES:2 * LANES] = proj(C_CMP + LANES, LANES)
    gate_ref[0] = jax.nn.sigmoid(proj(C_GATE, LANES))
    z_ref[0] = proj(C_Z, 3 * PAIR_D)
    sq = proj(C_SB, PAIR_D)
    sb_ref[0, :, 0:PAIR_D] = (sq * q_scale).astype(BF16)
    sb_ref[0, :, PAIR_D:3 * PAIR_D] = proj(C_SB + PAIR_D, 2 * PAIR_D).astype(BF16)

    cq = proj(C_CQ, Q_RANK_PAD)
    cqn = cq * lax.rsqrt(jnp.sum(cq * cq, axis=-1, keepdims=True) / MLA_Q_RANK + EPS) * qn_ref[...]
    q2 = _dot(cqn.astype(BF16), wuq_ref[...])
    mla_ref[0, :, 0:PAIR_D] = q2[:, 0:PAIR_D].astype(BF16)
    for p in range(PAIR_SLABS):
        r = _rope(q2[:, PAIR_D + p * LANES:PAIR_D + (p + 1) * LANES], c32, s32, MLA_ROPE)
        mla_ref[0, :, PAIR_D + p * LANES:PAIR_D + (p + 1) * LANES] = r.astype(BF16)
    ckv = proj(C_CKV, LANES)
    ckvn = ckv * lax.rsqrt(jnp.mean(ckv * ckv, axis=-1, keepdims=True) + EPS) * kvn_ref[...]
    kv2 = _dot(ckvn.astype(BF16), wukv_ref[...])
    mla_ref[0, :, 2 * PAIR_D:4 * PAIR_D] = kv2.astype(BF16)
    kr = _rope(proj(C_KR, LANES), c32, s32, MLA_ROPE)
    mla_ref[0, :, 4 * PAIR_D:4 * PAIR_D + LANES] = kr.astype(BF16)


def _in_projection(x, mod3, norm_g, w_in_p, q_norm_p, w_uq_p, kv_norm, w_ukv_p, tabs, ts=256):
    b, s, d = x.shape
    row = lambda n: pl.BlockSpec((1, ts, n), lambda i, j: (i, j, 0))
    full = lambda a: pl.BlockSpec(a.shape, lambda i, j: (0,) * a.ndim)
    outs = [(NSA_D, BF16), (4 * LANES, BF16), (2 * LANES, F32), (LANES, F32),
            (3 * PAIR_D, F32), (3 * PAIR_D, BF16), (4 * PAIR_D + LANES, BF16)]
    return pl.pallas_call(
        _inproj_kernel,
        grid=(b, s // ts),
        in_specs=[row(d), pl.BlockSpec((1, 3, d), lambda i, j: (i, 0, 0)), full(norm_g),
                  full(w_in_p), full(q_norm_p), full(w_uq_p), full(kv_norm), full(w_ukv_p),
                  row(LANES), row(LANES), row(LANES), row(LANES)],
        out_specs=[row(n) for n, _ in outs],
        out_shape=[jax.ShapeDtypeStruct((b, s, n), dt) for n, dt in outs],
        compiler_params=_cparams(("arbitrary", "arbitrary")),
        name="in_projection",
    )(x, mod3, norm_g, w_in_p, q_norm_p, w_uq_p, kv_norm, w_ukv_p, *tabs)


def _cmp_kernel(kc_ref, vc_ref, pos_ref, wkt_ref, wkb_ref, wk2_ref, wvt_ref, wvb_ref, wv2_ref, o_ref):
    def compress(chunks, pos_row, wt_ref, wb_ref, w2_ref):
        a = _dot((chunks + pos_ref[pos_row:pos_row + 1, :]).astype(BF16), wt_ref[...])
        bm = _dot((chunks + pos_ref[pos_row + 1:pos_row + 2, :]).astype(BF16), wb_ref[...])
        n = chunks.shape[0]
        pre = a + pltpu.roll(bm, n - 1, 0)
        return _dot(jax.nn.silu(pre).astype(BF16), w2_ref[...])

    o_ref[0, :, 0:LANES] = compress(kc_ref[0], 0, wkt_ref, wkb_ref, wk2_ref).astype(BF16)
    o_ref[0, :, LANES:2 * LANES] = compress(vc_ref[0], 2, wvt_ref, wvb_ref, wv2_ref).astype(BF16)


def _nsa_compress(kc_chunks, vc_chunks, pos4, wk, wv):
    b, nc, cw = kc_chunks.shape
    full = lambda a: pl.BlockSpec(a.shape, lambda i: (0,) * a.ndim)
    chunk = pl.BlockSpec((1, nc, cw), lambda i: (i, 0, 0))
    return pl.pallas_call(
        _cmp_kernel,
        grid=(b,),
        in_specs=[chunk, chunk, full(pos4)] + [full(w) for w in wk] + [full(w) for w in wv],
        out_specs=pl.BlockSpec((1, nc, 2 * LANES), lambda i: (i, 0, 0)),
        out_shape=jax.ShapeDtypeStruct((b, nc, 2 * LANES), BF16),
        compiler_params=_cparams(("arbitrary",)),
        name="nsa_compress",
    )(kc_chunks, vc_chunks, pos4, *wk, *wv)


def _nsa_kernel(q_ref, gate_ref, kcvc_ref, kv_ref, ov_ref, o_ref, *, tq, tk, top):
    t0 = pl.program_id(1) * tq
    nc = kcvc_ref.shape[1]
    ns = ov_ref.shape[1]
    hq = NSA_HPG * tq
    q = q_ref[0]
    gate = gate_ref[0]
    t_col = t0 + _iota((tq, 1), 0)
    lane = _iota((1, LANES), 1)
    kc = kcvc_ref[0, :, 0:LANES]
    vc = kcvc_ref[0, :, LANES:2 * LANES]
    vis = (_iota((1, nc), 1) * CMP_STRIDE + (CMP_LEN - 1)) <= t_col
    j_row = _iota((1, ns), 1)
    cur = t_col // SEL_BLOCK
    forced = (j_row == 0) | (j_row == cur) | (j_row == cur - 1)
    valid = j_row * SEL_BLOCK <= t_col
    win = WINDOW + tq
    w0 = pl.multiple_of(jnp.maximum(t0 - WINDOW, 0), tq)
    n_kt = (t0 + tq + tk - 1) // tk

    def stack3(mask2d):
        return jnp.concatenate([mask2d] * NSA_HPG, axis=0)

    comb = []
    for g in range(NSA_GROUPS):
        gsel = (lane // HEAD_DIM) == g
        qg = jnp.concatenate([jnp.where(gsel, q[:, i * LANES:(i + 1) * LANES], jnp.zeros((), BF16))
                              for i in range(NSA_HPG)], axis=0)

        s = _dot_nt(qg, kc) + stack3(jnp.where(vis, 0.0, -jnp.inf))
        m = jnp.maximum(jnp.max(s, axis=-1, keepdims=True), M_FLOOR)
        p = jnp.exp(s - m)
        l = jnp.sum(p, axis=-1, keepdims=True)
        pc = p * jnp.where(l > 0.0, 1.0 / l, 0.0)
        o_c = _dot(pc.astype(BF16), vc)

        imp = pc[0:tq] + pc[tq:2 * tq] + pc[2 * tq:3 * tq]
        hi = imp.astype(BF16)
        lo = (imp - hi.astype(F32)).astype(BF16)
        imp_sel = _dot(hi, ov_ref[...]) + _dot(lo, ov_ref[...])
        impf = jnp.where(valid, jnp.where(forced, SEL_FORCE, imp_sel), -SEL_FORCE)
        rank = jnp.zeros((tq, ns), F32)
        for jp in range(ns):
            col = impf[:, jp:jp + 1]
            beats = (col > impf) | ((col == impf) & (j_row > jp))
            rank = rank + jnp.where(beats, 1.0, 0.0)
        sel = jnp.where(rank < top, 1.0, 0.0).astype(BF16)

        def sel_body(kt, carry):
            m_i, l_i, acc = carry
            k0 = pl.multiple_of(kt * tk, tk)
            k = kv_ref[0, pl.ds(k0, tk), 0:LANES]
            v = kv_ref[0, pl.ds(k0, tk), LANES:2 * LANES]
            expand = (_iota((ns, tk), 0) == (k0 // SEL_BLOCK + _iota((ns, tk), 1) // SEL_BLOCK))
            selx = _dot(sel, jnp.where(expand, 1.0, 0.0).astype(BF16))
            tok = k0 + _iota((1, tk), 1)
            msk = (selx > 0.5) & (tok <= t_col)
            sc = _dot_nt(qg, k) + stack3(jnp.where(msk, 0.0, -jnp.inf))
            m_n = jnp.maximum(m_i, jnp.max(sc, axis=-1, keepdims=True))
            alpha = jnp.exp(m_i - m_n)
            pp = jnp.exp(sc - m_n)
            l_n = alpha * l_i + jnp.sum(pp, axis=-1, keepdims=True)
            acc_n = alpha * acc + _dot(pp.astype(BF16), v)
            return m_n, l_n, acc_n

        init = (jnp.full((hq, 1), M_FLOOR, F32), jnp.zeros((hq, 1), F32), jnp.zeros((hq, LANES), F32))
        _, l_s, acc_s = lax.fori_loop(0, n_kt, sel_body, init)
        o_s = acc_s / l_s

        kw = kv_ref[0, pl.ds(w0, win), 2 * LANES:3 * LANES]
        vw = kv_ref[0, pl.ds(w0, win), 3 * LANES:4 * LANES]
        tokw = w0 + _iota((1, win), 1)
        mw = (tokw <= t_col) & (tokw > t_col - WINDOW)
        sw = _dot_nt(qg, kw) + stack3(jnp.where(mw, 0.0, -jnp.inf))
        pw = jnp.exp(sw - jnp.max(sw, axis=-1, keepdims=True))
        o_w = _dot(pw.astype(BF16), vw) / jnp.sum(pw, axis=-1, keepdims=True)

        heads = []
        for i in range(NSA_HPG):
            c = (g * NSA_HPG + i) * 3
            rows = slice(i * tq, (i + 1) * tq)
            heads.append(gate[:, c:c + 1] * o_c[rows] + gate[:, c + 1:c + 2] * o_s[rows]
                         + gate[:, c + 2:c + 3] * o_w[rows])
        comb.append(heads)

    for i in range(NSA_HPG):
        o_ref[0, :, i * LANES:(i + 1) * LANES] = jnp.where(lane < HEAD_DIM, comb[0][i], comb[1][i])


def _nsa_attention(nq, gate, kcvc, nkv, overlap, tq=128, tk=512):
    b, s, _ = nq.shape
    nc = kcvc.shape[1]
    ns = overlap.shape[1]
    kern = functools.partial(_nsa_kernel, tq=tq, tk=tk, top=min(SEL_TOPK, ns))
    return pl.pallas_call(
        kern,
        grid=(b, s // tq),
        in_specs=[pl.BlockSpec((1, tq, NSA_D), lambda i, j: (i, j, 0)),
                  pl.BlockSpec((1, tq, LANES), lambda i, j: (i, j, 0)),
                  pl.BlockSpec((1, nc, 2 * LANES), lambda i, j: (i, 0, 0)),
                  pl.BlockSpec((1, s, 4 * LANES), lambda i, j: (i, 0, 0)),
                  pl.BlockSpec(overlap.shape, lambda i, j: (0, 0))],
        out_specs=pl.BlockSpec((1, tq, NSA_D), lambda i, j: (i, j, 0)),
        out_shape=jax.ShapeDtypeStruct((b, s, NSA_D), F32),
        compiler_params=_cparams(("arbitrary", "arbitrary")),
        name="nsa_attention",
    )(nq, gate, kcvc, nkv, overlap)


def _sb_kernel(q_ref, k_ref, v_ref, o_ref, *, tq, tk, n_heads):
    pair = pl.program_id(1)
    t0 = pl.program_id(2) * tq
    q = q_ref[0]
    t_col = t0 + _iota((tq, 1), 0)
    lane = _iota((1, LANES), 1)
    later = jnp.where(_iota((tk, tk), 0) > _iota((tk, tk), 1), 1.0, 0.0).astype(BF16)
    n_kt = (t0 + tq + tk - 1) // tk

    def head(a):
        qm = jnp.where((lane // HEAD_DIM) == a, q, jnp.zeros((), BF16))

        def body(i, carry):
            run, acc = carry
            k0 = pl.multiple_of((n_kt - 1 - i) * tk, tk)
            k = k_ref[0, pl.ds(k0, tk), :]
            v = v_ref[0, pl.ds(k0, tk), :]
            z = _dot_nt(qm, k)
            msk = (k0 + _iota((1, tk), 1)) < t_col
            lneg = jax.nn.log_sigmoid(-z)
            lm = jnp.where(msk, lneg, 0.0)
            hi = lm.astype(BF16)
            lo = (lm - hi.astype(F32)).astype(BF16)
            between = _dot(hi, later) + _dot(lo, later) + run
            a_w = jnp.where(msk, jnp.exp(lneg + z + between), 0.0)
            acc = acc + _dot(a_w.astype(BF16), v)
            run = run + jnp.sum(lm, axis=-1, keepdims=True)
            return run, acc

        init = (jnp.zeros((tq, 1), F32), jnp.zeros((tq, LANES), F32))
        return lax.fori_loop(0, n_kt, body, init)[1]

    o_ref[0] = jnp.where(lane < HEAD_DIM, head(0), 0.0)

    @pl.when(pair * 2 + 1 < n_heads)
    def _():
        o_ref[0] = jnp.where(lane < HEAD_DIM, o_ref[0], head(1))


def _sb_attention(sb, tq=256, tk=256):
    b, s, _ = sb.shape
    kern = functools.partial(_sb_kernel, tq=tq, tk=tk, n_heads=SB_HEADS)
    return pl.pallas_call(
        kern,
        grid=(b, PAIR_SLABS, s // tq),
        in_specs=[pl.BlockSpec((1, tq, LANES), lambda i, p, j: (i, j, p)),
                  pl.BlockSpec((1, s, LANES), lambda i, p, j: (i, 0, PAIR_SLABS + p)),
                  pl.BlockSpec((1, s, LANES), lambda i, p, j: (i, 0, 2 * PAIR_SLABS + p))],
        out_specs=pl.BlockSpec((1, tq, LANES), lambda i, p, j: (i, j, p)),
        out_shape=jax.ShapeDtypeStruct((b, s, PAIR_D), F32),
        compiler_params=_cparams(("arbitrary", "arbitrary", "arbitrary")),
        name="sb_attention",
    )(sb, sb, sb)


def _mla_kernel(qn_ref, qr_ref, kn_ref, kpe_ref, v_ref, o_ref, *, tq, tk, n_heads):
    pair = pl.program_id(1)
    t0 = pl.program_id(2) * tq
    q = jnp.concatenate([qn_ref[0], qr_ref[0]], axis=1)
    t_col = t0 + _iota((tq, 1), 0)
    lane = _iota((1, LANES), 1)
    lane2 = _iota((1, 2 * LANES), 1)
    scale = (MLA_NOPE + MLA_ROPE) ** -0.5
    n_kt = (t0 + tq + tk - 1) // tk

    lane_head = jnp.where(lane2 < LANES, lane2 // MLA_NOPE, (lane2 - LANES) // MLA_ROPE)

    def head(a):
        qm = jnp.where(lane_head == a, q, jnp.zeros((), BF16))

        def body(kt, carry):
            m_i, l_i, acc = carry
            k0 = pl.multiple_of(kt * tk, tk)
            k = jnp.concatenate([kn_ref[0, pl.ds(k0, tk), :], kpe_ref[0, pl.ds(k0, tk), :]], axis=1)
            v = v_ref[0, pl.ds(k0, tk), :]
            msk = (k0 + _iota((1, tk), 1)) <= t_col
            sc = jnp.where(msk, _dot_nt(qm, k) * scale, -jnp.inf)
            m_n = jnp.maximum(m_i, jnp.max(sc, axis=-1, keepdims=True))
            alpha = jnp.exp(m_i - m_n)
            pp = jnp.exp(sc - m_n)
            l_n = alpha * l_i + jnp.sum(pp, axis=-1, keepdims=True)
            acc_n = alpha * acc + _dot(pp.astype(BF16), v)
            return m_n, l_n, acc_n

        init = (jnp.full((tq, 1), M_FLOOR, F32), jnp.zeros((tq, 1), F32), jnp.zeros((tq, LANES), F32))
        _, l_f, acc_f = lax.fori_loop(0, n_kt, body, init)
        return acc_f / l_f

    o_ref[0] = jnp.where(lane < MLA_V, head(0), 0.0)

    @pl.when(pair * 2 + 1 < n_heads)
    def _():
        o_ref[0] = jnp.where(lane < MLA_V, o_ref[0], head(1))


def _mla_attention(mla, tq=256, tk=256):
    b, s, _ = mla.shape
    kern = functools.partial(_mla_kernel, tq=tq, tk=tk, n_heads=MLA_HEADS)
    ps = PAIR_SLABS
    return pl.pallas_call(
        kern,
        grid=(b, ps, s // tq),
        in_specs=[pl.BlockSpec((1, tq, LANES), lambda i, p, j: (i, j, p)),
                  pl.BlockSpec((1, tq, LANES), lambda i, p, j: (i, j, ps + p)),
                  pl.BlockSpec((1, s, LANES), lambda i, p, j: (i, 0, 2 * ps + p)),
                  pl.BlockSpec((1, s, LANES), lambda i, p, j: (i, 0, 4 * ps)),
                  pl.BlockSpec((1, s, LANES), lambda i, p, j: (i, 0, 3 * ps + p))],
        out_specs=pl.BlockSpec((1, tq, LANES), lambda i, p, j: (i, j, p)),
        out_shape=jax.ShapeDtypeStruct((b, s, PAIR_D), F32),
        compiler_params=_cparams(("arbitrary", "arbitrary", "arbitrary")),
        name="mla_attention",
    )(mla, mla, mla, mla, mla)


def _outproj_kernel(oa_ref, ob_ref, oc_ref, z_ref, x_ref, mod_ref, w_ref, fg_ref, o_ref, *, final):
    mix = jnp.concatenate([oa_ref[0], ob_ref[0], oc_ref[0]], axis=1) * jax.nn.silu(z_ref[0])
    y = _dot(mix.astype(BF16), w_ref[...])
    xn = x_ref[0] + mod_ref[0, 2:3, :] * y
    if final:
        ms = jnp.mean(xn * xn, axis=-1, keepdims=True)
        xn = xn * lax.rsqrt(ms + EPS) * fg_ref[...]
    o_ref[0] = xn


def _out_projection(oa, ob, oc, z, x, mod3, w_out_p, final_g, final, ts=256):
    b, s, d = x.shape
    row = lambda n: pl.BlockSpec((1, ts, n), lambda i, j: (i, j, 0))
    full = lambda a: pl.BlockSpec(a.shape, lambda i, j: (0,) * a.ndim)
    return pl.pallas_call(
        functools.partial(_outproj_kernel, final=final),
        grid=(b, s // ts),
        in_specs=[row(NSA_D), row(PAIR_D), row(PAIR_D), row(3 * PAIR_D), row(d),
                  pl.BlockSpec((1, 3, d), lambda i, j: (i, 0, 0)), full(w_out_p), full(final_g)],
        out_specs=row(d),
        out_shape=jax.ShapeDtypeStruct((b, s, d), F32),
        compiler_params=_cparams(("arbitrary", "arbitrary")),
        name="out_projection",
    )(oa, ob, oc, z, x, mod3, w_out_p, final_g)


def _overlap_matrix(s):
    n_cmp = (s - CMP_LEN) // CMP_STRIDE + 1
    n_sel = s // SEL_BLOCK
    c_start = np.arange(s // CMP_STRIDE) * CMP_STRIDE
    s_start = np.arange(n_sel) * SEL_BLOCK
    ov = ((c_start[:, None] < s_start[None, :] + SEL_BLOCK)
          & (c_start[:, None] + CMP_LEN > s_start[None, :])
          & (np.arange(s // CMP_STRIDE)[:, None] < n_cmp))
    return jnp.asarray(ov.astype(np.float32), dtype=BF16)


def _mixer_layer(x, mod3, tabs, overlap, norm_g, w_in, pos_k, pos_v, ck_w1, ck_w2, cv_w1, cv_w2,
                 q_norm, w_uq, kv_norm, w_ukv, w_out, final_g, final):
    b, s, d = x.shape
    w_in_p = _take_cols(w_in, _in_cols()).astype(BF16)
    w_uq_p = jnp.pad(_take_cols(w_uq, _uq_cols()), ((0, Q_RANK_PAD - MLA_Q_RANK), (0, 0))).astype(BF16)
    w_ukv_p = _take_cols(w_ukv, _ukv_cols()).astype(BF16)
    q_norm_p = jnp.pad(q_norm, (0, Q_RANK_PAD - MLA_Q_RANK)).reshape(1, Q_RANK_PAD)
    w_out_p = _take_cols(w_out.T, _out_rows()).T.astype(BF16)

    nq, nkv, cmp_tok, gate, z, sb, mla = _in_projection(
        x, mod3, norm_g.reshape(1, d), w_in_p, q_norm_p, w_uq_p, kv_norm.reshape(1, -1), w_ukv_p, tabs)

    chunk_w = CMP_STRIDE * LANES
    kc_chunks = cmp_tok[:, :, 0:LANES].reshape(b, s // CMP_STRIDE, chunk_w)
    vc_chunks = cmp_tok[:, :, LANES:].reshape(b, s // CMP_STRIDE, chunk_w)
    pos4 = jnp.concatenate([_cmp_pos(pos_k), _cmp_pos(pos_v)], axis=0)
    kcvc = _nsa_compress(kc_chunks, vc_chunks, pos4, _cmp_weights(ck_w1, ck_w2), _cmp_weights(cv_w1, cv_w2))

    o_a = _nsa_attention(nq, gate, kcvc, nkv, overlap)
    o_b = _sb_attention(sb)
    o_c = _mla_attention(mla)
    return _out_projection(o_a, o_b, o_c, z, x, mod3, w_out_p, final_g.reshape(1, d), final)


def kernel(x, c, positions, ada_w, ada_b, norm_g, w_in, nsa_pos_k, nsa_pos_v, nsa_ck_w1, nsa_ck_w2,
           nsa_cv_w1, nsa_cv_w2, mla_q_norm, mla_w_uq, mla_kv_norm, mla_w_ukv, w_out, final_norm):
    b, s, d = x.shape
    depth = w_in.shape[0]
    mod = _adaln_mod(c, ada_w, ada_b).reshape(depth, b, 3, d)
    tabs = _rope_tables(positions)
    overlap = _overlap_matrix(s)
    for l in range(depth):
        x = _mixer_layer(x, mod[l], tabs, overlap, norm_g[l], w_in[l], nsa_pos_k[l], nsa_pos_v[l],
                         nsa_ck_w1[l], nsa_ck_w2[l], nsa_cv_w1[l], nsa_cv_w2[l],
                         mla_q_norm[l], mla_w_uq[l], mla_kv_norm[l], mla_w_ukv[l], w_out[l],
                         final_norm, final=(l == depth - 1))
    return x
```

```python
import functools

import numpy as np
import jax
import jax.numpy as jnp
from jax import lax
from jax.experimental import pallas as pl
from jax.experimental.pallas import tpu as pltpu

F32 = jnp.float32
BF16 = jnp.bfloat16

LANES = 128
HEAD_DIM = 64
NSA_HEADS = 6
NSA_GROUPS = 2
NSA_HPG = NSA_HEADS // NSA_GROUPS
SB_HEADS = 5
MLA_HEADS = 5
MLA_NOPE = 64
MLA_ROPE = 32
MLA_V = 64
MLA_Q_RANK = 192
MLA_KV_RANK = 128
CMP_LEN = 32
CMP_STRIDE = 16
CMP_HIDDEN = 128
SEL_BLOCK = 64
SEL_TOPK = 16
WINDOW = 512
ROPE_THETA = 10000.0
EPS = 1e-6
SEL_FORCE = 1e9
M_FLOOR = -1e30
SB_DEAD = -110.0

NSA_D = NSA_HEADS * HEAD_DIM
PAIR_SLABS = 3
PAIR_D = PAIR_SLABS * LANES
Q_RANK_PAD = 256

KEY_TILE = 256
SWEEP = 2 * KEY_TILE
VT_SLABS = 8

C_NQ = 0
C_NKV = C_NQ + NSA_D
C_CMP = C_NKV + 4 * LANES
C_GATE = C_CMP + 2 * LANES
C_Z = C_GATE + LANES
C_SB = C_Z + 3 * PAIR_D
C_CQ = C_SB + 3 * PAIR_D
C_CKV = C_CQ + Q_RANK_PAD
C_KR = C_CKV + LANES
D_IN_PAD = C_KR + LANES

VMEM_LIMIT = 52 * 1024 * 1024


def _cparams(sem):
    return pltpu.CompilerParams(dimension_semantics=sem, vmem_limit_bytes=VMEM_LIMIT)


def _dot(a, b):
    return jnp.dot(a, b, preferred_element_type=F32)


def _dot_nt(a, b):
    return lax.dot_general(a, b, (((1,), (1,)), ((), ())), preferred_element_type=F32)


def _iota(shape, dim):
    return lax.broadcasted_iota(jnp.int32, shape, dim)


def _rope(x, cos, sin_signed, width):
    half = width // 2
    lane = _iota(x.shape, 1)
    first = (lane % width) < half
    rot = jnp.where(first, pltpu.roll(x, LANES - half, 1), pltpu.roll(x, half, 1))
    return x * cos + rot * sin_signed


def _hi_lo(x):
    hi = x.astype(BF16)
    return hi, (x - hi.astype(F32)).astype(BF16)


def _lanes(parts):
    return parts[0] if len(parts) == 1 else jnp.concatenate(parts, axis=1)


def _in_cols():
    widths = (NSA_D, 128, 128, 128, 128, 128, 128, NSA_HEADS * 3, NSA_D,
              320, 320, 320, 320, MLA_Q_RANK, MLA_KV_RANK, MLA_ROPE, 320)
    off = np.concatenate([[0], np.cumsum(widths)])
    (o_nq, o_kc, o_vc, o_ks, o_vs, o_kw, o_vw, o_gate, o_nz,
     o_sq, o_sk, o_sv, o_sz, o_cq, o_ckv, o_kr, o_mz) = [int(v) for v in off[:-1]]
    cols = -np.ones((D_IN_PAD,), np.int64)

    def put(dst, src, n):
        cols[dst:dst + n] = src + np.arange(n)

    def put_nsa(dst, src):
        for i in range(NSA_HPG):
            for g in range(NSA_GROUPS):
                put(dst + i * LANES + g * HEAD_DIM, src + (g * NSA_HPG + i) * HEAD_DIM, HEAD_DIM)

    put_nsa(C_NQ, o_nq)
    put(C_NKV, o_ks, 128)
    put(C_NKV + 128, o_vs, 128)
    put(C_NKV + 256, o_kw, 128)
    put(C_NKV + 384, o_vw, 128)
    put(C_CMP, o_kc, 128)
    put(C_CMP + 128, o_vc, 128)
    put(C_GATE, o_gate, NSA_HEADS * 3)
    put_nsa(C_Z, o_nz)
    put(C_Z + PAIR_D, o_sz, 320)
    put(C_Z + 2 * PAIR_D, o_mz, 320)
    put(C_SB, o_sq, 320)
    put(C_SB + PAIR_D, o_sk, 320)
    put(C_SB + 2 * PAIR_D, o_sv, 320)
    put(C_CQ, o_cq, MLA_Q_RANK)
    put(C_CKV, o_ckv, MLA_KV_RANK)
    put(C_KR, o_kr, MLA_ROPE)
    put(C_KR + MLA_ROPE, o_kr, MLA_ROPE)
    return cols


def _take_cols(w, cols):
    cols = np.asarray(cols)
    g = jnp.take(w, jnp.asarray(np.maximum(cols, 0)), axis=1)
    return jnp.where(jnp.asarray(cols >= 0)[None, :], g, 0.0)


def _uq_cols():
    cols = -np.ones((2 * PAIR_D,), np.int64)
    for h in range(MLA_HEADS):
        p, a = divmod(h, 2)
        base = h * (MLA_NOPE + MLA_ROPE)
        cols[p * LANES + a * MLA_NOPE: p * LANES + (a + 1) * MLA_NOPE] = base + np.arange(MLA_NOPE)
        d = PAIR_D + p * LANES + a * MLA_ROPE
        cols[d: d + MLA_ROPE] = base + MLA_NOPE + np.arange(MLA_ROPE)
    return cols


def _ukv_cols():
    cols = -np.ones((2 * PAIR_D,), np.int64)
    for h in range(MLA_HEADS):
        base = h * (MLA_NOPE + MLA_V)
        cols[h * MLA_NOPE: (h + 1) * MLA_NOPE] = base + np.arange(MLA_NOPE)
        cols[PAIR_D + h * MLA_V: PAIR_D + (h + 1) * MLA_V] = base + MLA_NOPE + np.arange(MLA_V)
    return cols


def _out_rows():
    rows = -np.ones((3 * PAIR_D,), np.int64)
    for i in range(NSA_HPG):
        for g in range(NSA_GROUPS):
            d = i * LANES + g * HEAD_DIM
            rows[d:d + HEAD_DIM] = (g * NSA_HPG + i) * HEAD_DIM + np.arange(HEAD_DIM)
    rows[PAIR_D:PAIR_D + 320] = NSA_D + np.arange(320)
    rows[2 * PAIR_D:2 * PAIR_D + 320] = NSA_D + 320 + np.arange(320)
    return rows


def _cmp_weights(w1, w2):
    half = CMP_LEN // 2
    w1r = w1.reshape(CMP_LEN, HEAD_DIM, CMP_HIDDEN)
    z = jnp.zeros((half, HEAD_DIM, CMP_HIDDEN), w1.dtype)

    def build(part):
        g0 = jnp.concatenate([part, z], axis=1)
        g1 = jnp.concatenate([z, part], axis=1)
        return jnp.concatenate([g0.reshape(half * LANES, CMP_HIDDEN),
                                g1.reshape(half * LANES, CMP_HIDDEN)], axis=1)

    zz = jnp.zeros_like(w2)
    w2bd = jnp.concatenate([jnp.concatenate([w2, zz], axis=1),
                            jnp.concatenate([zz, w2], axis=1)], axis=0)
    return build(w1r[:half]).astype(BF16), build(w1r[half:]).astype(BF16), w2bd.astype(BF16)


def _cmp_pos(pos):
    half = CMP_LEN // 2
    tiled = jnp.concatenate([pos, pos], axis=1)
    return jnp.stack([tiled[:half].reshape(half * LANES), tiled[half:].reshape(half * LANES)])


def _mod_kernel(c_ref, w_ref, b_ref, o_ref):
    a = jax.nn.silu(c_ref[...])
    o_ref[0] = _dot(a, w_ref[0]) + b_ref[0]


def _adaln_mod(c, ada_w, ada_b):
    depth, d, n = ada_w.shape
    b = c.shape[0]
    tn = 1024
    return pl.pallas_call(
        _mod_kernel,
        grid=(depth, n // tn),
        in_specs=[pl.BlockSpec((b, d), lambda l, j: (0, 0)),
                  pl.BlockSpec((1, d, tn), lambda l, j: (l, 0, j)),
                  pl.BlockSpec((1, 1, tn), lambda l, j: (l, 0, j))],
        out_specs=pl.BlockSpec((1, b, tn), lambda l, j: (l, 0, j)),
        out_shape=jax.ShapeDtypeStruct((depth, b, n), F32),
        compiler_params=_cparams(("arbitrary", "arbitrary")),
        name="adaln_mod",
    )(c, ada_w, ada_b.reshape(depth, 1, n))


def _rope_kernel(pos_ref, inv_ref, c64_ref, s64_ref, c32_ref, s32_ref):
    pos = pos_ref[0].astype(F32)
    lane = _iota((1, LANES), 1)
    a64 = pos * inv_ref[0:1, :]
    a32 = pos * inv_ref[1:2, :]
    c64_ref[0] = jnp.cos(a64)
    s64 = jnp.sin(a64)
    s64_ref[0] = jnp.where((lane % HEAD_DIM) < HEAD_DIM // 2, -s64, s64)
    c32_ref[0] = jnp.cos(a32)
    s32 = jnp.sin(a32)
    s32_ref[0] = jnp.where((lane % MLA_ROPE) < MLA_ROPE // 2, -s32, s32)


def _rope_tables(positions):
    b, s = positions.shape
    ts = 512

    def inv(half):
        v = ROPE_THETA ** (-jnp.arange(half, dtype=F32) / half)
        return jnp.tile(v, LANES // half)

    inv_tab = jnp.zeros((8, LANES), F32).at[0].set(inv(HEAD_DIM // 2)).at[1].set(inv(MLA_ROPE // 2))
    tab = jax.ShapeDtypeStruct((b, s, LANES), F32)
    spec = pl.BlockSpec((1, ts, LANES), lambda i, j: (i, j, 0))
    return pl.pallas_call(
        _rope_kernel,
        grid=(b, s // ts),
        in_specs=[pl.BlockSpec((1, ts, 1), lambda i, j: (i, j, 0)),
                  pl.BlockSpec((8, LANES), lambda i, j: (0, 0))],
        out_specs=[spec] * 4,
        out_shape=[tab] * 4,
        compiler_params=_cparams(("arbitrary", "arbitrary")),
        name="rope_tables",
    )(positions.reshape(b, s, 1), inv_tab)


def _inproj_kernel(x_ref, mod_ref, g_ref, w_ref, qn_ref, wuq_ref, kvn_ref, wukv_ref,
                   c64_ref, s64_ref, c32_ref, s32_ref,
                   nq_ref, nk_ref, cmp_ref, gatet_ref, z_ref, sb_ref, mla_ref, vt_ref):
    x = x_ref[0]
    ms = jnp.mean(x * x, axis=-1, keepdims=True)
    y = x * lax.rsqrt(ms + EPS) * g_ref[...]
    h = y * (1.0 + mod_ref[0, 1:2, :]) + mod_ref[0, 0:1, :]
    hb = h.astype(BF16)

    def proj(c0, n):
        return _dot(hb, w_ref[:, c0:c0 + n])

    def put_vt(slab, val):
        vt_ref[0, slab, 0] = val.T.astype(BF16)

    c64, s64 = c64_ref[0], s64_ref[0]
    c32, s32 = c32_ref[0], s32_ref[0]
    q_scale = HEAD_DIM ** -0.5

    for i in range(NSA_HPG):
        q = _rope(proj(C_NQ + i * LANES, LANES), c64, s64, HEAD_DIM)
        nq_ref[0, :, i * LANES:(i + 1) * LANES] = (q * q_scale).astype(BF16)
    nk_ref[0, :, 0:LANES] = _rope(proj(C_NKV, LANES), c64, s64, HEAD_DIM).astype(BF16)
    put_vt(0, proj(C_NKV + LANES, LANES))
    nk_ref[0, :, LANES:2 * LANES] = _rope(proj(C_NKV + 2 * LANES, LANES), c64, s64, HEAD_DIM).astype(BF16)
    put_vt(1, proj(C_NKV + 3 * LANES, LANES))
    cmp_ref[0, :, 0:LANES] = _rope(proj(C_CMP, LANES), c64, s64, HEAD_DIM)
    cmp_ref[0, :, LANES:2 * LANES] = proj(C_CMP + LANES, LANES)
    gatet_ref[0] = jax.nn.sigmoid(proj(C_GATE, LANES)).T
    z_ref[0] = proj(C_Z, 3 * PAIR_D)
    sb_ref[0, :, 0:PAIR_D] = (proj(C_SB, PAIR_D) * q_scale).astype(BF16)
    sb_ref[0, :, PAIR_D:2 * PAIR_D] = proj(C_SB + PAIR_D, PAIR_D).astype(BF16)
    sv = proj(C_SB + 2 * PAIR_D, PAIR_D)
    for p in range(PAIR_SLABS):
        put_vt(2 + p, sv[:, p * LANES:(p + 1) * LANES])

    cq = proj(C_CQ, Q_RANK_PAD)
    cqn = cq * lax.rsqrt(jnp.sum(cq * cq, axis=-1, keepdims=True) / MLA_Q_RANK + EPS) * qn_ref[...]
    q2 = _dot(cqn.astype(BF16), wuq_ref[...])
    mla_ref[0, :, 0:PAIR_D] = q2[:, 0:PAIR_D].astype(BF16)
    for p in range(PAIR_SLABS):
        r = _rope(q2[:, PAIR_D + p * LANES:PAIR_D + (p + 1) * LANES], c32, s32, MLA_ROPE)
        mla_ref[0, :, PAIR_D + p * LANES:PAIR_D + (p + 1) * LANES] = r.astype(BF16)
    ckv = proj(C_CKV, LANES)
    ckvn = ckv * lax.rsqrt(jnp.mean(ckv * ckv, axis=-1, keepdims=True) + EPS) * kvn_ref[...]
    kv2 = _dot(ckvn.astype(BF16), wukv_ref[...])
    mla_ref[0, :, 2 * PAIR_D:3 * PAIR_D] = kv2[:, 0:PAIR_D].astype(BF16)
    for p in range(PAIR_SLABS):
        put_vt(2 + PAIR_SLABS + p, kv2[:, PAIR_D + p * LANES:PAIR_D + (p + 1) * LANES])
    kr = _rope(proj(C_KR, LANES), c32, s32, MLA_ROPE)
    mla_ref[0, :, 3 * PAIR_D:3 * PAIR_D + LANES] = kr.astype(BF16)


def _in_projection(x, mod3, norm_g, w_in_p, q_norm_p, w_uq_p, kv_norm, w_ukv_p, tabs):
    b, s, d = x.shape
    ts = KEY_TILE
    row = lambda n: pl.BlockSpec((1, ts, n), lambda i, j: (i, j, 0))
    full = lambda a: pl.BlockSpec(a.shape, lambda i, j: (0,) * a.ndim)
    outs = [(NSA_D, BF16), (2 * LANES, BF16), (2 * LANES, F32), None,
            (3 * PAIR_D, F32), (2 * PAIR_D, BF16), (3 * PAIR_D + LANES, BF16), None]
    out_specs = [row(o[0]) if o else None for o in outs]
    out_shape = [jax.ShapeDtypeStruct((b, s, o[0]), o[1]) if o else None for o in outs]
    out_specs[3] = pl.BlockSpec((1, LANES, ts), lambda i, j: (i, 0, j))
    out_shape[3] = jax.ShapeDtypeStruct((b, LANES, s), F32)
    out_specs[7] = pl.BlockSpec((1, VT_SLABS, 1, LANES, ts), lambda i, j: (i, 0, j, 0, 0))
    out_shape[7] = jax.ShapeDtypeStruct((b, VT_SLABS, s // ts, LANES, ts), BF16)
    return pl.pallas_call(
        _inproj_kernel,
        grid=(b, s // ts),
        in_specs=[row(d), pl.BlockSpec((1, 3, d), lambda i, j: (i, 0, 0)), full(norm_g),
                  full(w_in_p), full(q_norm_p), full(w_uq_p), full(kv_norm), full(w_ukv_p),
                  row(LANES), row(LANES), row(LANES), row(LANES)],
        out_specs=out_specs,
        out_shape=out_shape,
        compiler_params=_cparams(("arbitrary", "arbitrary")),
        name="in_projection",
    )(x, mod3, norm_g, w_in_p, q_norm_p, w_uq_p, kv_norm, w_ukv_p, *tabs)


def _cmp_kernel(kc_ref, vc_ref, pos_ref, wkt_ref, wkb_ref, wk2_ref, wvt_ref, wvb_ref, wv2_ref,
                kc_out, vct_out):
    def compress(chunks, pos_row, wt_ref, wb_ref, w2_ref):
        a = _dot((chunks + pos_ref[pos_row:pos_row + 1, :]).astype(BF16), wt_ref[...])
        bm = _dot((chunks + pos_ref[pos_row + 1:pos_row + 2, :]).astype(BF16), wb_ref[...])
        n = chunks.shape[0]
        pre = a + pltpu.roll(bm, n - 1, 0)
        return _dot(jax.nn.silu(pre).astype(BF16), w2_ref[...])

    kc_out[0] = compress(kc_ref[0], 0, wkt_ref, wkb_ref, wk2_ref).astype(BF16)
    vct_out[0] = compress(vc_ref[0], 2, wvt_ref, wvb_ref, wv2_ref).T.astype(BF16)


def _nsa_compress(kc_chunks, vc_chunks, pos4, wk, wv):
    b, nc, cw = kc_chunks.shape
    full = lambda a: pl.BlockSpec(a.shape, lambda i: (0,) * a.ndim)
    chunk = pl.BlockSpec((1, nc, cw), lambda i: (i, 0, 0))
    return pl.pallas_call(
        _cmp_kernel,
        grid=(b,),
        in_specs=[chunk, chunk, full(pos4)] + [full(w) for w in wk] + [full(w) for w in wv],
        out_specs=[pl.BlockSpec((1, nc, LANES), lambda i: (i, 0, 0)),
                   pl.BlockSpec((1, LANES, nc), lambda i: (i, 0, 0))],
        out_shape=[jax.ShapeDtypeStruct((b, nc, LANES), BF16),
                   jax.ShapeDtypeStruct((b, LANES, nc), BF16)],
        compiler_params=_cparams(("arbitrary",)),
        name="nsa_compress",
    )(kc_chunks, vc_chunks, pos4, *wk, *wv)


def _nsa_kernel(q_ref, gt_ref, kc_ref, vct_ref, k_ref, vt_ref, ovt_ref, o_ref, s_buf, *, tq, top):
    t0 = pl.program_id(1) * tq
    nc = kc_ref.shape[1]
    ns = ovt_ref.shape[0]
    per_sweep = SWEEP // KEY_TILE
    win_tiles = -(-(WINDOW + tq + max(KEY_TILE - tq, 0)) // KEY_TILE)
    q = q_ref[0]
    lane = _iota((1, LANES), 1)
    qs = jnp.concatenate([jnp.where((lane // HEAD_DIM) == g, q[:, i * LANES:(i + 1) * LANES],
                                    jnp.zeros((), BF16))
                          for g in range(NSA_GROUPS) for i in range(NSA_HPG)], axis=0)
    t_row = t0 + _iota((1, tq), 1)

    def all_heads(a):
        return _lanes([a] * NSA_HEADS)

    def by_group(a2):
        return _lanes([a2[:, 0:tq]] * NSA_HPG + [a2[:, tq:2 * tq]] * NSA_HPG)

    vis = (_iota((nc, 1), 0) * CMP_STRIDE + (CMP_LEN - 1)) <= t_row
    s = _dot_nt(kc_ref[0], qs) + all_heads(jnp.where(vis, 0.0, -jnp.inf))
    mx = jnp.maximum(jnp.max(s, axis=0, keepdims=True), M_FLOOR)
    p = jnp.exp(s - mx)
    l = jnp.sum(p, axis=0, keepdims=True)
    pc = p * jnp.where(l > 0.0, 1.0 / l, 0.0)
    o_c = _dot(vct_ref[0], pc.astype(BF16))

    j_col = _iota((ns, 1), 0)
    cur = t_row // SEL_BLOCK
    forced = (j_col == 0) | (j_col == cur) | (j_col == cur - 1)
    valid = j_col * SEL_BLOCK <= t_row
    sel = []
    for g in range(NSA_GROUPS):
        c0 = g * NSA_HPG * tq
        imp = pc[:, c0:c0 + tq] + pc[:, c0 + tq:c0 + 2 * tq] + pc[:, c0 + 2 * tq:c0 + 3 * tq]
        hi, lo = _hi_lo(imp)
        imp_sel = _dot(ovt_ref[...], hi) + _dot(ovt_ref[...], lo)
        impf = jnp.where(valid, jnp.where(forced, SEL_FORCE, imp_sel), -SEL_FORCE)
        rank = jnp.zeros((ns, tq), F32)
        for jp in range(ns):
            r = impf[jp:jp + 1, :]
            beats = (r > impf) | ((r == impf) & (j_col > jp))
            rank = rank + jnp.where(beats, 1.0, 0.0)
        sel.append(jnp.where(rank < top, 1.0, 0.0).astype(BF16))
    sel2 = _lanes(sel)
    t_row2 = _lanes([t_row, t_row])

    n_it = (t0 + tq + SWEEP - 1) // SWEEP

    def sel_scores(it, m_run):
        k0 = pl.multiple_of(it * SWEEP, SWEEP)
        expand = _iota((SWEEP, ns), 1) == (k0 + _iota((SWEEP, ns), 0)) // SEL_BLOCK
        selx = _dot(jnp.where(expand, 1.0, 0.0).astype(BF16), sel2)
        ok = (selx > 0.5) & ((k0 + _iota((SWEEP, 1), 0)) <= t_row2)
        sc = _dot_nt(k_ref[0, pl.ds(k0, SWEEP), 0:LANES], qs) + by_group(jnp.where(ok, 0.0, -jnp.inf))
        s_buf[it] = sc
        return jnp.maximum(m_run, jnp.max(sc, axis=0, keepdims=True))

    m_s = lax.fori_loop(0, n_it, sel_scores, jnp.full((1, NSA_HEADS * tq), M_FLOOR, F32))

    def sel_values(it, carry):
        l_i, acc = carry
        pp = jnp.exp(s_buf[it] - m_s)
        pb = pp.astype(BF16)
        for h in range(per_sweep):
            acc = acc + _dot(vt_ref[0, 0, it * per_sweep + h], pb[h * KEY_TILE:(h + 1) * KEY_TILE])
        return l_i + jnp.sum(pp, axis=0, keepdims=True), acc

    l_s, acc_s = lax.fori_loop(0, n_it, sel_values, (jnp.zeros((1, NSA_HEADS * tq), F32),
                                                     jnp.zeros((LANES, NSA_HEADS * tq), F32)))
    o_s = acc_s / l_s

    wt = jnp.maximum(t0 - WINDOW, 0) // KEY_TILE
    w0 = pl.multiple_of(wt * KEY_TILE, KEY_TILE)
    wlen = win_tiles * KEY_TILE
    tokw = w0 + _iota((wlen, 1), 0)
    okw = (tokw <= t_row) & (tokw > t_row - WINDOW)
    sw = _dot_nt(k_ref[0, pl.ds(w0, wlen), LANES:2 * LANES], qs) + all_heads(jnp.where(okw, 0.0, -jnp.inf))
    pw = jnp.exp(sw - jnp.max(sw, axis=0, keepdims=True))
    pwb = pw.astype(BF16)
    acc_w = jnp.zeros((LANES, NSA_HEADS * tq), F32)
    for h in range(win_tiles):
        acc_w = acc_w + _dot(vt_ref[0, 1, wt + h], pwb[h * KEY_TILE:(h + 1) * KEY_TILE])
    o_w = acc_w / jnp.sum(pw, axis=0, keepdims=True)

    gt = gt_ref[0]

    def gate_row(br):
        return _lanes([gt[(g * NSA_HPG + i) * 3 + br:(g * NSA_HPG + i) * 3 + br + 1, :]
                       for g in range(NSA_GROUPS) for i in range(NSA_HPG)])

    mix = gate_row(0) * o_c + gate_row(1) * o_s + gate_row(2) * o_w
    row = _iota((LANES, 1), 0)
    for i in range(NSA_HPG):
        slab_t = jnp.where(row < HEAD_DIM, mix[:, i * tq:(i + 1) * tq],
                           mix[:, (NSA_HPG + i) * tq:(NSA_HPG + i + 1) * tq])
        o_ref[0, :, i * LANES:(i + 1) * LANES] = slab_t.T


def _nsa_attention(nq, gate_t, kc, vct, nk, vt, overlap_t, tq=128):
    b, s, _ = nq.shape
    nc = kc.shape[1]
    ns = overlap_t.shape[0]
    kern = functools.partial(_nsa_kernel, tq=tq, top=min(SEL_TOPK, ns))
    return pl.pallas_call(
        kern,
        grid=(b, s // tq),
        in_specs=[pl.BlockSpec((1, tq, NSA_D), lambda i, j: (i, j, 0)),
                  pl.BlockSpec((1, LANES, tq), lambda i, j: (i, 0, j)),
                  pl.BlockSpec((1, nc, LANES), lambda i, j: (i, 0, 0)),
                  pl.BlockSpec((1, LANES, nc), lambda i, j: (i, 0, 0)),
                  pl.BlockSpec((1, s, 2 * LANES), lambda i, j: (i, 0, 0)),
                  pl.BlockSpec((1, 2, s // KEY_TILE, LANES, KEY_TILE), lambda i, j: (i, 0, 0, 0, 0)),
                  pl.BlockSpec(overlap_t.shape, lambda i, j: (0, 0))],
        out_specs=pl.BlockSpec((1, tq, NSA_D), lambda i, j: (i, j, 0)),
        out_shape=jax.ShapeDtypeStruct((b, s, NSA_D), F32),
        scratch_shapes=[pltpu.VMEM((s // SWEEP, SWEEP, NSA_HEADS * tq), F32)],
        compiler_params=_cparams(("arbitrary", "arbitrary")),
        name="nsa_attention",
    )(nq, gate_t, kc, vct, nk, vt, overlap_t)


def _sb_kernel(q_ref, k_ref, vt_ref, o_ref, *, tq, n_heads):
    tk = KEY_TILE
    pair = pl.program_id(1)
    qi = pl.program_id(2)
    q = q_ref[0]
    lane = _iota((1, LANES), 1)
    t_row1 = qi * tq + _iota((1, tq), 1)
    row = _iota((LANES, 1), 0)
    later = jnp.where(_iota((tk, tk), 1) > _iota((tk, tk), 0), 1.0, 0.0).astype(BF16)

    def run(nh):
        qs = jnp.concatenate([jnp.where((lane // HEAD_DIM) == a, q, jnp.zeros((), BF16))
                              for a in range(nh)], axis=0)
        t_row = _lanes([t_row1] * nh)

        def tile(kt, run_row, acc, diagonal):
            k0 = pl.multiple_of(kt * tk, tk)
            z = _dot_nt(k_ref[0, pl.ds(k0, tk), :], qs)
            lneg = jax.nn.log_sigmoid(-z)
            if diagonal:
                msk = (k0 + _iota((tk, 1), 0)) < t_row
                lm = jnp.where(msk, lneg, 0.0)
            else:
                lm = lneg
            hi, lo = _hi_lo(lm)
            between = _dot(later, hi) + _dot(later, lo) + run_row
            a_w = jnp.exp(lneg + z + between)
            if diagonal:
                a_w = jnp.where(msk, a_w, 0.0)
            acc = acc + _dot(vt_ref[0, 0, kt], a_w.astype(BF16))
            return run_row + jnp.sum(lm, axis=0, keepdims=True), acc

        run_row, acc = tile(qi, jnp.zeros((1, nh * tq), F32), jnp.zeros((LANES, nh * tq), F32), True)

        def alive(c):
            return (c[0] >= 0) & (jnp.max(c[1]) > SB_DEAD)

        def step(c):
            r, a = tile(c[0], c[1], c[2], False)
            return c[0] - 1, r, a

        acc = lax.while_loop(alive, step, (qi - 1, run_row, acc))[2]
        if nh == 2:
            o_t = jnp.where(row < HEAD_DIM, acc[:, 0:tq], acc[:, tq:2 * tq])
        else:
            o_t = jnp.where(row < HEAD_DIM, acc, 0.0)
        o_ref[0] = o_t.T

    @pl.when(pair * 2 + 1 < n_heads)
    def _():
        run(2)

    @pl.when(pair * 2 + 1 >= n_heads)
    def _():
        run(1)


def _sb_attention(sb, vt):
    b, s, _ = sb.shape
    tq = KEY_TILE
    kern = functools.partial(_sb_kernel, tq=tq, n_heads=SB_HEADS)
    return pl.pallas_call(
        kern,
        grid=(b, PAIR_SLABS, s // tq),
        in_specs=[pl.BlockSpec((1, tq, LANES), lambda i, p, j: (i, j, p)),
                  pl.BlockSpec((1, s, LANES), lambda i, p, j: (i, 0, PAIR_SLABS + p)),
                  pl.BlockSpec((1, 1, s // KEY_TILE, LANES, KEY_TILE), lambda i, p, j: (i, 2 + p, 0, 0, 0))],
        out_specs=pl.BlockSpec((1, tq, LANES), lambda i, p, j: (i, j, p)),
        out_shape=jax.ShapeDtypeStruct((b, s, PAIR_D), F32),
        compiler_params=_cparams(("arbitrary", "arbitrary", "arbitrary")),
        name="sb_attention",
    )(sb, sb, vt)


def _mla_kernel(qn_ref, qr_ref, kn_ref, kpe_ref, vt_ref, o_ref, s_buf, *, tq, n_heads):
    pair = pl.program_id(1)
    t0 = pl.program_id(2) * tq
    per_sweep = SWEEP // KEY_TILE
    q = jnp.concatenate([qn_ref[0], qr_ref[0]], axis=1)
    lane2 = _iota((1, 2 * LANES), 1)
    lane_head = jnp.where(lane2 < LANES, lane2 // MLA_NOPE, (lane2 - LANES) // MLA_ROPE)
    t_row1 = t0 + _iota((1, tq), 1)
    row = _iota((LANES, 1), 0)
    scale = (MLA_NOPE + MLA_ROPE) ** -0.5
    n_full = t0 // SWEEP
    n_it = (t0 + tq + SWEEP - 1) // SWEEP

    def run(nh):
        m = nh * tq
        qs = jnp.concatenate([jnp.where(lane_head == a, q, jnp.zeros((), BF16))
                              for a in range(nh)], axis=0)
        t_row = _lanes([t_row1] * nh)

        def scores(it, m_run, diagonal):
            k0 = pl.multiple_of(it * SWEEP, SWEEP)
            k = jnp.concatenate([kn_ref[0, pl.ds(k0, SWEEP), :], kpe_ref[0, pl.ds(k0, SWEEP), :]], axis=1)
            sc = _dot_nt(k, qs) * scale
            if diagonal:
                sc = jnp.where((k0 + _iota((SWEEP, 1), 0)) <= t_row, sc, -jnp.inf)
            s_buf[it, :, 0:m] = sc
            return jnp.maximum(m_run, jnp.max(sc, axis=0, keepdims=True))

        m_f = lax.fori_loop(0, n_full, lambda it, c: scores(it, c, False), jnp.full((1, m), M_FLOOR, F32))
        m_f = lax.fori_loop(n_full, n_it, lambda it, c: scores(it, c, True), m_f)

        def values(it, carry):
            l_i, acc = carry
            pp = jnp.exp(s_buf[it, :, 0:m] - m_f)
            pb = pp.astype(BF16)
            for h in range(per_sweep):
                acc = acc + _dot(vt_ref[0, 0, it * per_sweep + h], pb[h * KEY_TILE:(h + 1) * KEY_TILE])
            return l_i + jnp.sum(pp, axis=0, keepdims=True), acc

        l_f, acc = lax.fori_loop(0, n_it, values, (jnp.zeros((1, m), F32), jnp.zeros((LANES, m), F32)))
        o = acc / l_f
        if nh == 2:
            o_t = jnp.where(row < MLA_V, o[:, 0:tq], o[:, tq:2 * tq])
        else:
            o_t = jnp.where(row < MLA_V, o, 0.0)
        o_ref[0] = o_t.T

    @pl.when(pair * 2 + 1 < n_heads)
    def _():
        run(2)

    @pl.when(pair * 2 + 1 >= n_heads)
    def _():
        run(1)


def _mla_attention(mla, vt, tq=256):
    b, s, _ = mla.shape
    kern = functools.partial(_mla_kernel, tq=tq, n_heads=MLA_HEADS)
    ps = PAIR_SLABS
    return pl.pallas_call(
        kern,
        grid=(b, ps, s // tq),
        in_specs=[pl.BlockSpec((1, tq, LANES), lambda i, p, j: (i, j, p)),
                  pl.BlockSpec((1, tq, LANES), lambda i, p, j: (i, j, ps + p)),
                  pl.BlockSpec((1, s, LANES), lambda i, p, j: (i, 0, 2 * ps + p)),
                  pl.BlockSpec((1, s, LANES), lambda i, p, j: (i, 0, 3 * ps)),
                  pl.BlockSpec((1, 1, s // KEY_TILE, LANES, KEY_TILE),
                               lambda i, p, j: (i, 2 + ps + p, 0, 0, 0))],
        out_specs=pl.BlockSpec((1, tq, LANES), lambda i, p, j: (i, j, p)),
        out_shape=jax.ShapeDtypeStruct((b, s, PAIR_D), F32),
        scratch_shapes=[pltpu.VMEM((s // SWEEP, SWEEP, 2 * tq), F32)],
        compiler_params=_cparams(("arbitrary", "arbitrary", "arbitrary")),
        name="mla_attention",
    )(mla, mla, mla, mla, vt)


def _outproj_kernel(oa_ref, ob_ref, oc_ref, z_ref, x_ref, mod_ref, w_ref, fg_ref, o_ref, *, final):
    mix = jnp.concatenate([oa_ref[0], ob_ref[0], oc_ref[0]], axis=1) * jax.nn.silu(z_ref[0])
    y = _dot(mix.astype(BF16), w_ref[...])
    xn = x_ref[0] + mod_ref[0, 2:3, :] * y
    if final:
        ms = jnp.mean(xn * xn, axis=-1, keepdims=True)
        xn = xn * lax.rsqrt(ms + EPS) * fg_ref[...]
    o_ref[0] = xn


def _out_projection(oa, ob, oc, z, x, mod3, w_out_p, final_g, final, ts=256):
    b, s, d = x.shape
    row = lambda n: pl.BlockSpec((1, ts, n), lambda i, j: (i, j, 0))
    full = lambda a: pl.BlockSpec(a.shape, lambda i, j: (0,) * a.ndim)
    return pl.pallas_call(
        functools.partial(_outproj_kernel, final=final),
        grid=(b, s // ts),
        in_specs=[row(NSA_D), row(PAIR_D), row(PAIR_D), row(3 * PAIR_D), row(d),
                  pl.BlockSpec((1, 3, d), lambda i, j: (i, 0, 0)), full(w_out_p), full(final_g)],
        out_specs=row(d),
        out_shape=jax.ShapeDtypeStruct((b, s, d), F32),
        compiler_params=_cparams(("arbitrary", "arbitrary")),
        name="out_projection",
    )(oa, ob, oc, z, x, mod3, w_out_p, final_g)


def _overlap_matrix_t(s):
    n_cmp = (s - CMP_LEN) // CMP_STRIDE + 1
    n_sel = s // SEL_BLOCK
    c_start = np.arange(s // CMP_STRIDE) * CMP_STRIDE
    s_start = np.arange(n_sel) * SEL_BLOCK
    ov = ((c_start[None, :] < s_start[:, None] + SEL_BLOCK)
          & (c_start[None, :] + CMP_LEN > s_start[:, None])
          & (np.arange(s // CMP_STRIDE)[None, :] < n_cmp))
    return jnp.asarray(ov.astype(np.float32), dtype=BF16)


def _mixer_layer(x, mod3, tabs, overlap_t, norm_g, w_in, pos_k, pos_v, ck_w1, ck_w2, cv_w1, cv_w2,
                 q_norm, w_uq, kv_norm, w_ukv, w_out, final_g, final):
    b, s, d = x.shape
    w_in_p = _take_cols(w_in, _in_cols()).astype(BF16)
    w_uq_p = jnp.pad(_take_cols(w_uq, _uq_cols()), ((0, Q_RANK_PAD - MLA_Q_RANK), (0, 0))).astype(BF16)
    w_ukv_p = _take_cols(w_ukv, _ukv_cols()).astype(BF16)
    q_norm_p = jnp.pad(q_norm, (0, Q_RANK_PAD - MLA_Q_RANK)).reshape(1, Q_RANK_PAD)
    w_out_p = _take_cols(w_out.T, _out_rows()).T.astype(BF16)

    nq, nk, cmp_tok, gate_t, z, sb, mla, vt = _in_projection(
        x, mod3, norm_g.reshape(1, d), w_in_p, q_norm_p, w_uq_p, kv_norm.reshape(1, -1), w_ukv_p, tabs)

    chunk_w = CMP_STRIDE * LANES
    kc_chunks = cmp_tok[:, :, 0:LANES].reshape(b, s // CMP_STRIDE, chunk_w)
    vc_chunks = cmp_tok[:, :, LANES:].reshape(b, s // CMP_STRIDE, chunk_w)
    pos4 = jnp.concatenate([_cmp_pos(pos_k), _cmp_pos(pos_v)], axis=0)
    kc, vct = _nsa_compress(kc_chunks, vc_chunks, pos4,
                            _cmp_weights(ck_w1, ck_w2), _cmp_weights(cv_w1, cv_w2))

    o_a = _nsa_attention(nq, gate_t, kc, vct, nk, vt, overlap_t)
    o_b = _sb_attention(sb, vt)
    o_c = _mla_attention(mla, vt)
    return _out_projection(o_a, o_b, o_c, z, x, mod3, w_out_p, final_g.reshape(1, d), final)


def kernel(x, c, positions, ada_w, ada_b, norm_g, w_in, nsa_pos_k, nsa_pos_v, nsa_ck_w1, nsa_ck_w2,
           nsa_cv_w1, nsa_cv_w2, mla_q_norm, mla_w_uq, mla_kv_norm, mla_w_ukv, w_out, final_norm):
    b, s, d = x.shape
    depth = w_in.shape[0]
    assert s % SWEEP == 0 and s >= WINDOW + 2 * KEY_TILE
    mod = _adaln_mod(c, ada_w, ada_b).reshape(depth, b, 3, d)
    tabs = _rope_tables(positions)
    overlap_t = _overlap_matrix_t(s)
    for l in range(depth):
        x = _mixer_layer(x, mod[l], tabs, overlap_t, norm_g[l], w_in[l], nsa_pos_k[l], nsa_pos_v[l],
                         nsa_ck_w1[l], nsa_ck_w2[l], nsa_cv_w1[l], nsa_cv_w2[l],
                         mla_q_norm[l], mla_w_uq[l], mla_kv_norm[l], mla_w_ukv[l], w_out[l],
                         final_norm, final=(l == depth - 1))
    return x
```

```python
import functools

import numpy as np
import jax
import jax.numpy as jnp
from jax import lax
from jax.experimental import pallas as pl
from jax.experimental.pallas import tpu as pltpu

F32 = jnp.float32
BF16 = jnp.bfloat16

LANES = 128
HEAD_DIM = 64
NSA_HEADS = 6
NSA_GROUPS = 2
NSA_HPG = NSA_HEADS // NSA_GROUPS
SB_HEADS = 5
MLA_HEADS = 5
MLA_NOPE = 64
MLA_ROPE = 32
MLA_V = 64
MLA_Q_RANK = 192
MLA_KV_RANK = 128
CMP_LEN = 32
CMP_STRIDE = 16
CMP_HIDDEN = 128
SEL_BLOCK = 64
SEL_TOPK = 16
WINDOW = 512
ROPE_THETA = 10000.0
EPS = 1e-6
SEL_FORCE = 1e9
M_FLOOR = -1e30
SB_DEAD = -110.0
LOG2E = 1.4426950408889634
NEG_MASK = -1e30
ONES_ROWS = 16

NSA_D = NSA_HEADS * HEAD_DIM
PAIR_SLABS = 3
PAIR_D = PAIR_SLABS * LANES
Q_RANK_PAD = 256

KEY_TILE = 256
SWEEP = 2 * KEY_TILE
VT_SLABS = 8

C_NQ = 0
C_NKV = C_NQ + NSA_D
C_CMP = C_NKV + 4 * LANES
C_GATE = C_CMP + 2 * LANES
C_Z = C_GATE + LANES
C_SB = C_Z + 3 * PAIR_D
C_CQ = C_SB + 3 * PAIR_D
C_CKV = C_CQ + Q_RANK_PAD
C_KR = C_CKV + LANES
D_IN_PAD = C_KR + LANES

VMEM_LIMIT = 52 * 1024 * 1024


def _cparams(sem):
    return pltpu.CompilerParams(dimension_semantics=sem, vmem_limit_bytes=VMEM_LIMIT)


def _dot(a, b):
    return jnp.dot(a, b, preferred_element_type=F32)


def _dot_nt(a, b):
    return lax.dot_general(a, b, (((1,), (1,)), ((), ())), preferred_element_type=F32)


def _iota(shape, dim):
    return lax.broadcasted_iota(jnp.int32, shape, dim)


def _rope(x, cos, sin_signed, width):
    half = width // 2
    lane = _iota(x.shape, 1)
    first = (lane % width) < half
    rot = jnp.where(first, pltpu.roll(x, LANES - half, 1), pltpu.roll(x, half, 1))
    return x * cos + rot * sin_signed


def _hi_lo(x):
    hi = x.astype(BF16)
    return hi, (x - hi.astype(F32)).astype(BF16)


def _lanes(parts):
    return parts[0] if len(parts) == 1 else jnp.concatenate(parts, axis=1)


def _sweep_loop(lo, hi, step, carry):
    pairs = (hi - lo) // 2
    carry = lax.fori_loop(0, pairs, lambda i, c: step(lo + 2 * i + 1, step(lo + 2 * i, c)), carry)
    return lax.fori_loop(lo + 2 * pairs, hi, step, carry)


def _in_cols():
    widths = (NSA_D, 128, 128, 128, 128, 128, 128, NSA_HEADS * 3, NSA_D,
              320, 320, 320, 320, MLA_Q_RANK, MLA_KV_RANK, MLA_ROPE, 320)
    off = np.concatenate([[0], np.cumsum(widths)])
    (o_nq, o_kc, o_vc, o_ks, o_vs, o_kw, o_vw, o_gate, o_nz,
     o_sq, o_sk, o_sv, o_sz, o_cq, o_ckv, o_kr, o_mz) = [int(v) for v in off[:-1]]
    cols = -np.ones((D_IN_PAD,), np.int64)

    def put(dst, src, n):
        cols[dst:dst + n] = src + np.arange(n)

    def put_nsa(dst, src):
        for i in range(NSA_HPG):
            for g in range(NSA_GROUPS):
                put(dst + i * LANES + g * HEAD_DIM, src + (g * NSA_HPG + i) * HEAD_DIM, HEAD_DIM)

    put_nsa(C_NQ, o_nq)
    put(C_NKV, o_ks, 128)
    put(C_NKV + 128, o_vs, 128)
    put(C_NKV + 256, o_kw, 128)
    put(C_NKV + 384, o_vw, 128)
    put(C_CMP, o_kc, 128)
    put(C_CMP + 128, o_vc, 128)
    put(C_GATE, o_gate, NSA_HEADS * 3)
    put_nsa(C_Z, o_nz)
    put(C_Z + PAIR_D, o_sz, 320)
    put(C_Z + 2 * PAIR_D, o_mz, 320)
    put(C_SB, o_sq, 320)
    put(C_SB + PAIR_D, o_sk, 320)
    put(C_SB + 2 * PAIR_D, o_sv, 320)
    put(C_CQ, o_cq, MLA_Q_RANK)
    put(C_CKV, o_ckv, MLA_KV_RANK)
    put(C_KR, o_kr, MLA_ROPE)
    put(C_KR + MLA_ROPE, o_kr, MLA_ROPE)
    return cols


def _take_cols(w, cols):
    cols = np.asarray(cols)
    g = jnp.take(w, jnp.asarray(np.maximum(cols, 0)), axis=1)
    return jnp.where(jnp.asarray(cols >= 0)[None, :], g, 0.0)


def _uq_cols():
    cols = -np.ones((2 * PAIR_D,), np.int64)
    for h in range(MLA_HEADS):
        p, a = divmod(h, 2)
        base = h * (MLA_NOPE + MLA_ROPE)
        cols[p * LANES + a * MLA_NOPE: p * LANES + (a + 1) * MLA_NOPE] = base + np.arange(MLA_NOPE)
        d = PAIR_D + p * LANES + a * MLA_ROPE
        cols[d: d + MLA_ROPE] = base + MLA_NOPE + np.arange(MLA_ROPE)
    return cols


def _ukv_cols():
    cols = -np.ones((2 * PAIR_D,), np.int64)
    for h in range(MLA_HEADS):
        base = h * (MLA_NOPE + MLA_V)
        cols[h * MLA_NOPE: (h + 1) * MLA_NOPE] = base + np.arange(MLA_NOPE)
        cols[PAIR_D + h * MLA_V: PAIR_D + (h + 1) * MLA_V] = base + MLA_NOPE + np.arange(MLA_V)
    return cols


def _out_rows():
    rows = -np.ones((3 * PAIR_D,), np.int64)
    for i in range(NSA_HPG):
        for g in range(NSA_GROUPS):
            d = i * LANES + g * HEAD_DIM
            rows[d:d + HEAD_DIM] = (g * NSA_HPG + i) * HEAD_DIM + np.arange(HEAD_DIM)
    rows[PAIR_D:PAIR_D + 320] = NSA_D + np.arange(320)
    rows[2 * PAIR_D:2 * PAIR_D + 320] = NSA_D + 320 + np.arange(320)
    return rows


def _cmp_weights(w1, w2):
    half = CMP_LEN // 2
    w1r = w1.reshape(CMP_LEN, HEAD_DIM, CMP_HIDDEN)
    z = jnp.zeros((half, HEAD_DIM, CMP_HIDDEN), w1.dtype)

    def build(part):
        g0 = jnp.concatenate([part, z], axis=1)
        g1 = jnp.concatenate([z, part], axis=1)
        return jnp.concatenate([g0.reshape(half * LANES, CMP_HIDDEN),
                                g1.reshape(half * LANES, CMP_HIDDEN)], axis=1)

    zz = jnp.zeros_like(w2)
    w2bd = jnp.concatenate([jnp.concatenate([w2, zz], axis=1),
                            jnp.concatenate([zz, w2], axis=1)], axis=0)
    return build(w1r[:half]).astype(BF16), build(w1r[half:]).astype(BF16), w2bd.astype(BF16)


def _cmp_pos(pos):
    half = CMP_LEN // 2
    tiled = jnp.concatenate([pos, pos], axis=1)
    return jnp.stack([tiled[:half].reshape(half * LANES), tiled[half:].reshape(half * LANES)])


def _mod_kernel(c_ref, w_ref, b_ref, o_ref):
    a = jax.nn.silu(c_ref[...])
    o_ref[0] = _dot(a, w_ref[0]) + b_ref[0]


def _adaln_mod(c, ada_w, ada_b):
    depth, d, n = ada_w.shape
    b = c.shape[0]
    tn = 1024
    return pl.pallas_call(
        _mod_kernel,
        grid=(depth, n // tn),
        in_specs=[pl.BlockSpec((b, d), lambda l, j: (0, 0)),
                  pl.BlockSpec((1, d, tn), lambda l, j: (l, 0, j)),
                  pl.BlockSpec((1, 1, tn), lambda l, j: (l, 0, j))],
        out_specs=pl.BlockSpec((1, b, tn), lambda l, j: (l, 0, j)),
        out_shape=jax.ShapeDtypeStruct((depth, b, n), F32),
        compiler_params=_cparams(("arbitrary", "arbitrary")),
        name="adaln_mod",
    )(c, ada_w, ada_b.reshape(depth, 1, n))


def _rope_kernel(pos_ref, inv_ref, c64_ref, s64_ref, c32_ref, s32_ref):
    pos = pos_ref[0].astype(F32)
    lane = _iota((1, LANES), 1)
    a64 = pos * inv_ref[0:1, :]
    a32 = pos * inv_ref[1:2, :]
    c64_ref[0] = jnp.cos(a64)
    s64 = jnp.sin(a64)
    s64_ref[0] = jnp.where((lane % HEAD_DIM) < HEAD_DIM // 2, -s64, s64)
    c32_ref[0] = jnp.cos(a32)
    s32 = jnp.sin(a32)
    s32_ref[0] = jnp.where((lane % MLA_ROPE) < MLA_ROPE // 2, -s32, s32)


def _rope_tables(positions):
    b, s = positions.shape
    ts = 512

    def inv(half):
        v = ROPE_THETA ** (-jnp.arange(half, dtype=F32) / half)
        return jnp.tile(v, LANES // half)

    inv_tab = jnp.zeros((8, LANES), F32).at[0].set(inv(HEAD_DIM // 2)).at[1].set(inv(MLA_ROPE // 2))
    tab = jax.ShapeDtypeStruct((b, s, LANES), F32)
    spec = pl.BlockSpec((1, ts, LANES), lambda i, j: (i, j, 0))
    return pl.pallas_call(
        _rope_kernel,
        grid=(b, s // ts),
        in_specs=[pl.BlockSpec((1, ts, 1), lambda i, j: (i, j, 0)),
                  pl.BlockSpec((8, LANES), lambda i, j: (0, 0))],
        out_specs=[spec] * 4,
        out_shape=[tab] * 4,
        compiler_params=_cparams(("arbitrary", "arbitrary")),
        name="rope_tables",
    )(positions.reshape(b, s, 1), inv_tab)


def _inproj_kernel(x_ref, mod_ref, g_ref, w_ref, qn_ref, wuq_ref, kvn_ref, wukv_ref,
                   c64_ref, s64_ref, c32_ref, s32_ref,
                   nq_ref, nk_ref, cmp_ref, gatet_ref, z_ref, sb_ref, mla_ref, vt_ref):
    x = x_ref[0]
    ms = jnp.mean(x * x, axis=-1, keepdims=True)
    y = x * lax.rsqrt(ms + EPS) * g_ref[...]
    h = y * (1.0 + mod_ref[0, 1:2, :]) + mod_ref[0, 0:1, :]
    hb = h.astype(BF16)

    def proj(c0, n):
        return _dot(hb, w_ref[:, c0:c0 + n])

    def slab(val, c0, j):
        return val[:, c0 + j * LANES:c0 + (j + 1) * LANES]

    def put_vt(slab_id, val):
        for sub in range(val.shape[0] // KEY_TILE):
            vt_ref[0, slab_id, sub] = val[sub * KEY_TILE:(sub + 1) * KEY_TILE].T.astype(BF16)

    c64, s64 = c64_ref[0], s64_ref[0]
    c32, s32 = c32_ref[0], s32_ref[0]
    q_scale = HEAD_DIM ** -0.5

    nsa = proj(C_NQ, C_Z - C_NQ)
    for i in range(NSA_HPG):
        q = _rope(slab(nsa, C_NQ, i), c64, s64, HEAD_DIM)
        nq_ref[0, :, i * LANES:(i + 1) * LANES] = (q * q_scale).astype(BF16)
    nk_ref[0, :, 0:LANES] = _rope(slab(nsa, C_NKV, 0), c64, s64, HEAD_DIM).astype(BF16)
    put_vt(0, slab(nsa, C_NKV, 1))
    nk_ref[0, :, LANES:2 * LANES] = _rope(slab(nsa, C_NKV, 2), c64, s64, HEAD_DIM).astype(BF16)
    put_vt(1, slab(nsa, C_NKV, 3))
    cmp_ref[0, :, 0:LANES] = _rope(slab(nsa, C_CMP, 0), c64, s64, HEAD_DIM)
    cmp_ref[0, :, LANES:2 * LANES] = slab(nsa, C_CMP, 1)
    gatet_ref[0] = jax.nn.sigmoid(slab(nsa, C_GATE, 0)).T
    z_ref[0] = proj(C_Z, 3 * PAIR_D)
    sbp = proj(C_SB, 3 * PAIR_D)
    sb_ref[0, :, 0:PAIR_D] = (sbp[:, 0:PAIR_D] * q_scale).astype(BF16)
    sb_ref[0, :, PAIR_D:2 * PAIR_D] = sbp[:, PAIR_D:2 * PAIR_D].astype(BF16)
    for p in range(PAIR_SLABS):
        put_vt(2 + p, slab(sbp, 2 * PAIR_D, p))

    lat = proj(C_CQ, D_IN_PAD - C_CQ)
    cq = lat[:, 0:Q_RANK_PAD]
    cqn = cq * lax.rsqrt(jnp.sum(cq * cq, axis=-1, keepdims=True) / MLA_Q_RANK + EPS) * qn_ref[...]
    q2 = _dot(cqn.astype(BF16), wuq_ref[...])
    mla_ref[0, :, 0:PAIR_D] = q2[:, 0:PAIR_D].astype(BF16)
    for p in range(PAIR_SLABS):
        r = _rope(q2[:, PAIR_D + p * LANES:PAIR_D + (p + 1) * LANES], c32, s32, MLA_ROPE)
        mla_ref[0, :, PAIR_D + p * LANES:PAIR_D + (p + 1) * LANES] = r.astype(BF16)
    ckv = lat[:, Q_RANK_PAD:Q_RANK_PAD + LANES]
    ckvn = ckv * lax.rsqrt(jnp.mean(ckv * ckv, axis=-1, keepdims=True) + EPS) * kvn_ref[...]
    kv2 = _dot(ckvn.astype(BF16), wukv_ref[...])
    mla_ref[0, :, 2 * PAIR_D:3 * PAIR_D] = kv2[:, 0:PAIR_D].astype(BF16)
    for p in range(PAIR_SLABS):
        put_vt(2 + PAIR_SLABS + p, slab(kv2, PAIR_D, p))
    kr = _rope(lat[:, Q_RANK_PAD + LANES:Q_RANK_PAD + 2 * LANES], c32, s32, MLA_ROPE)
    mla_ref[0, :, 3 * PAIR_D:3 * PAIR_D + LANES] = kr.astype(BF16)


def _in_projection(x, mod3, norm_g, w_in_p, q_norm_p, w_uq_p, kv_norm, w_ukv_p, tabs):
    b, s, d = x.shape
    ts = 2 * KEY_TILE
    row = lambda n: pl.BlockSpec((1, ts, n), lambda i, j: (i, j, 0))
    full = lambda a: pl.BlockSpec(a.shape, lambda i, j: (0,) * a.ndim, pipeline_mode=pl.Buffered(1))
    outs = [(NSA_D, BF16), (2 * LANES, BF16), (2 * LANES, F32), None,
            (3 * PAIR_D, F32), (2 * PAIR_D, BF16), (3 * PAIR_D + LANES, BF16), None]
    out_specs = [row(o[0]) if o else None for o in outs]
    out_shape = [jax.ShapeDtypeStruct((b, s, o[0]), o[1]) if o else None for o in outs]
    out_specs[3] = pl.BlockSpec((1, LANES, ts), lambda i, j: (i, 0, j))
    out_shape[3] = jax.ShapeDtypeStruct((b, LANES, s), F32)
    out_specs[7] = pl.BlockSpec((1, VT_SLABS, ts // KEY_TILE, LANES, KEY_TILE), lambda i, j: (i, 0, j, 0, 0))
    out_shape[7] = jax.ShapeDtypeStruct((b, VT_SLABS, s // KEY_TILE, LANES, KEY_TILE), BF16)
    return pl.pallas_call(
        _inproj_kernel,
        grid=(b, s // ts),
        in_specs=[row(d), pl.BlockSpec((1, 3, d), lambda i, j: (i, 0, 0)), full(norm_g),
                  full(w_in_p), full(q_norm_p), full(w_uq_p), full(kv_norm), full(w_ukv_p),
                  row(LANES), row(LANES), row(LANES), row(LANES)],
        out_specs=out_specs,
        out_shape=out_shape,
        compiler_params=_cparams(("arbitrary", "arbitrary")),
        name="in_projection",
    )(x, mod3, norm_g, w_in_p, q_norm_p, w_uq_p, kv_norm, w_ukv_p, *tabs)


def _cmp_kernel(kc_ref, vc_ref, pos_ref, wkt_ref, wkb_ref, wk2_ref, wvt_ref, wvb_ref, wv2_ref,
                kc_out, vct_out):
    def compress(chunks, pos_row, wt_ref, wb_ref, w2_ref):
        a = _dot((chunks + pos_ref[pos_row:pos_row + 1, :]).astype(BF16), wt_ref[...])
        bm = _dot((chunks + pos_ref[pos_row + 1:pos_row + 2, :]).astype(BF16), wb_ref[...])
        n = chunks.shape[0]
        pre = a + pltpu.roll(bm, n - 1, 0)
        return _dot(jax.nn.silu(pre).astype(BF16), w2_ref[...])

    kc_out[0] = compress(kc_ref[0], 0, wkt_ref, wkb_ref, wk2_ref).astype(BF16)
    vct_out[0] = compress(vc_ref[0], 2, wvt_ref, wvb_ref, wv2_ref).T.astype(BF16)


def _nsa_compress(kc_chunks, vc_chunks, pos4, wk, wv):
    b, nc, cw = kc_chunks.shape
    full = lambda a: pl.BlockSpec(a.shape, lambda i: (0,) * a.ndim)
    chunk = pl.BlockSpec((1, nc, cw), lambda i: (i, 0, 0))
    return pl.pallas_call(
        _cmp_kernel,
        grid=(b,),
        in_specs=[chunk, chunk, full(pos4)] + [full(w) for w in wk] + [full(w) for w in wv],
        out_specs=[pl.BlockSpec((1, nc, LANES), lambda i: (i, 0, 0)),
                   pl.BlockSpec((1, LANES, nc), lambda i: (i, 0, 0))],
        out_shape=[jax.ShapeDtypeStruct((b, nc, LANES), BF16),
                   jax.ShapeDtypeStruct((b, LANES, nc), BF16)],
        compiler_params=_cparams(("arbitrary",)),
        name="nsa_compress",
    )(kc_chunks, vc_chunks, pos4, *wk, *wv)


def _nsa_kernel(q_ref, gt_ref, kc_ref, vct_ref, k_ref, vt_ref, ovt_ref, oh_ref, o_ref, s_buf,
                *, tq, top):
    t0 = pl.program_id(1) * tq
    nc = kc_ref.shape[1]
    ns = ovt_ref.shape[0]
    per_sweep = SWEEP // KEY_TILE
    win_tiles = -(-(WINDOW + tq + max(KEY_TILE - tq, 0)) // KEY_TILE)
    ones_rows = jnp.ones((ONES_ROWS, KEY_TILE), BF16)
    m = NSA_HPG * tq
    q = q_ref[0]
    gt = gt_ref[0]
    lane = _iota((1, LANES), 1)
    t_row = t0 + _iota((1, tq), 1)

    def heads(a):
        return _lanes([a] * NSA_HPG)

    t_row3 = heads(t_row)
    vis_bias = jnp.where((_iota((nc, 1), 0) * CMP_STRIDE + (CMP_LEN - 1)) <= t_row, 0.0, -jnp.inf)
    j_col = _iota((ns, 1), 0)
    cur = t_row // SEL_BLOCK
    forced = (j_col == 0) | (j_col == cur) | (j_col == cur - 1)
    valid = j_col * SEL_BLOCK <= t_row
    n_full = t0 // SWEEP
    n_it = (t0 + tq + SWEEP - 1) // SWEEP
    wt = jnp.maximum(t0 - WINDOW, 0) // KEY_TILE
    w0 = pl.multiple_of(wt * KEY_TILE, KEY_TILE)
    wlen = win_tiles * KEY_TILE
    tokw = w0 + _iota((wlen, 1), 0)
    win_bias = jnp.where((tokw <= t_row) & (tokw > t_row - WINDOW), 0.0, -jnp.inf)

    def group(g):
        rows = slice(g * HEAD_DIM, (g + 1) * HEAD_DIM)
        qs = jnp.concatenate([jnp.where((lane // HEAD_DIM) == g, q[:, i * LANES:(i + 1) * LANES],
                                        jnp.zeros((), BF16)) for i in range(NSA_HPG)], axis=0)

        s = _dot_nt(kc_ref[0], qs) + heads(vis_bias)
        mx = jnp.maximum(jnp.max(s, axis=0, keepdims=True), M_FLOOR)
        p = jnp.exp(s - mx)
        l = jnp.sum(p, axis=0, keepdims=True)
        pc = p * jnp.where(l > 0.0, 1.0 / l, 0.0)
        o_c = _dot(vct_ref[0, rows, :], pc.astype(BF16))

        hi, lo = _hi_lo(pc[:, 0:tq] + pc[:, tq:2 * tq] + pc[:, 2 * tq:3 * tq])
        imp_sel = _dot(ovt_ref[...], hi) + _dot(ovt_ref[...], lo)
        rank = _rank(jnp.where(valid, jnp.where(forced, SEL_FORCE, imp_sel), -SEL_FORCE))

        member = jnp.concatenate([jnp.where(rank < top, 0.0, NEG_MASK), jnp.zeros((LANES - ns, tq), F32)],
                                 axis=0).T.astype(BF16)
        qs_aug = jnp.concatenate([qs, jnp.concatenate([member] * NSA_HPG, axis=0)], axis=1)

        def value_rows(slab_id, kt):
            return jnp.concatenate([vt_ref[0, slab_id, kt, rows, :], ones_rows], axis=0)

        def scores(it, m8, diagonal):
            k0 = pl.multiple_of(it * SWEEP, SWEEP)
            k_aug = jnp.concatenate([k_ref[0, pl.ds(k0, SWEEP), 0:LANES], oh_ref[pl.ds(k0, SWEEP), :]], axis=1)
            sc = _dot_nt(k_aug, qs_aug) * LOG2E
            if diagonal:
                sc = jnp.where((k0 + _iota((SWEEP, 1), 0)) <= t_row3, sc, -jnp.inf)
            s_buf[it] = sc
            return jnp.maximum(m8, jnp.max(sc.reshape(-1, 8, m), axis=0))

        m8 = _sweep_loop(0, n_full, lambda it, c: scores(it, c, False), jnp.full((8, m), M_FLOOR, F32))
        m8 = lax.fori_loop(n_full, n_it, lambda it, c: scores(it, c, True), m8)
        m_s = jnp.max(m8, axis=0, keepdims=True)

        def values(it, acc):
            pb = jnp.exp2(s_buf[it] - m_s).astype(BF16)
            for h in range(per_sweep):
                acc = acc + _dot(value_rows(0, it * per_sweep + h), pb[h * KEY_TILE:(h + 1) * KEY_TILE])
            return acc

        acc_s = _sweep_loop(0, n_it, values, jnp.zeros((HEAD_DIM + ONES_ROWS, m), F32))
        o_s = acc_s[0:HEAD_DIM] / acc_s[HEAD_DIM:HEAD_DIM + 1]

        sw = _dot_nt(k_ref[0, pl.ds(w0, wlen), LANES:2 * LANES], qs) + heads(win_bias)
        pwb = jnp.exp(sw - jnp.max(sw, axis=0, keepdims=True)).astype(BF16)
        acc_w = jnp.zeros((HEAD_DIM + ONES_ROWS, m), F32)
        for h in range(win_tiles):
            acc_w = acc_w + _dot(value_rows(1, wt + h), pwb[h * KEY_TILE:(h + 1) * KEY_TILE])
        o_w = acc_w[0:HEAD_DIM] / acc_w[HEAD_DIM:HEAD_DIM + 1]

        def gate_row(br):
            return _lanes([gt[(g * NSA_HPG + i) * 3 + br:(g * NSA_HPG + i) * 3 + br + 1, :]
                           for i in range(NSA_HPG)])

        return gate_row(0) * o_c + gate_row(1) * o_s + gate_row(2) * o_w

    mix = [group(g) for g in range(NSA_GROUPS)]
    for i in range(NSA_HPG):
        slab_t = jnp.concatenate([mix[g][:, i * tq:(i + 1) * tq] for g in range(NSA_GROUPS)], axis=0)
        o_ref[0, :, i * LANES:(i + 1) * LANES] = slab_t.T.astype(BF16)


def _rank(x):
    ns, tq = x.shape
    sub = _iota((8, 1), 0)
    parts = [x[r:r + 8] for r in range(0, ns, 8)]
    counts = [jnp.zeros((8, tq), F32) for _ in parts]
    for jp in range(ns):
        r = x[jp:jp + 1, :]
        for n, part in enumerate(parts):
            if 8 * n > jp:
                beats = r >= part
            elif 8 * n + 7 < jp:
                beats = r > part
            else:
                beats = (r > part) | ((r == part) & (sub + 8 * n > jp))
            counts[n] = counts[n] + jnp.where(beats, 1.0, 0.0)
    return jnp.concatenate(counts, axis=0)


def _nsa_attention(nq, gate_t, kc, vct, nk, vt, overlap_t, tq=256):
    b, s, _ = nq.shape
    nc = kc.shape[1]
    ns = overlap_t.shape[0]
    assert ns <= LANES
    block_of_key = np.arange(s)[:, None] // SEL_BLOCK == np.arange(LANES)[None, :]
    onehot = jnp.asarray(block_of_key.astype(np.float32), dtype=BF16)
    kern = functools.partial(_nsa_kernel, tq=tq, top=min(SEL_TOPK, ns))
    return pl.pallas_call(
        kern,
        grid=(b, s // tq),
        in_specs=[pl.BlockSpec((1, tq, NSA_D), lambda i, j: (i, j, 0)),
                  pl.BlockSpec((1, LANES, tq), lambda i, j: (i, 0, j)),
                  pl.BlockSpec((1, nc, LANES), lambda i, j: (i, 0, 0)),
                  pl.BlockSpec((1, LANES, nc), lambda i, j: (i, 0, 0)),
                  pl.BlockSpec((1, s, 2 * LANES), lambda i, j: (i, 0, 0)),
                  pl.BlockSpec((1, 2, s // KEY_TILE, LANES, KEY_TILE), lambda i, j: (i, 0, 0, 0, 0)),
                  pl.BlockSpec(overlap_t.shape, lambda i, j: (0, 0)),
                  pl.BlockSpec(onehot.shape, lambda i, j: (0, 0))],
        out_specs=pl.BlockSpec((1, tq, NSA_D), lambda i, j: (i, j, 0)),
        out_shape=jax.ShapeDtypeStruct((b, s, NSA_D), BF16),
        scratch_shapes=[pltpu.VMEM((s // SWEEP, SWEEP, NSA_HPG * tq), F32)],
        compiler_params=_cparams(("arbitrary", "arbitrary")),
        name="nsa_attention",
    )(nq, gate_t, kc, vct, nk, vt, overlap_t, onehot)


def _sb_kernel(q_ref, k_ref, vt_ref, o_ref, *, tq, n_heads):
    tk = KEY_TILE
    pair = pl.program_id(1)
    qi = pl.program_id(2)
    q = q_ref[0]
    lane = _iota((1, LANES), 1)
    t_row1 = qi * tq + _iota((1, tq), 1)
    row = _iota((LANES, 1), 0)
    later = jnp.where(_iota((tk, tk), 1) > _iota((tk, tk), 0), 1.0, 0.0).astype(BF16)

    def run(nh):
        qs = jnp.concatenate([jnp.where((lane // HEAD_DIM) == a, q, jnp.zeros((), BF16))
                              for a in range(nh)], axis=0)
        t_row = _lanes([t_row1] * nh)

        def tile(kt, run_row, acc, diagonal):
            k0 = pl.multiple_of(kt * tk, tk)
            z = _dot_nt(k_ref[0, pl.ds(k0, tk), :], qs)
            lneg = -(jnp.maximum(z, 0.0) + jnp.log1p(jnp.exp(-jnp.abs(z))))
            if diagonal:
                msk = (k0 + _iota((tk, 1), 0)) < t_row
                lm = jnp.where(msk, lneg, 0.0)
            else:
                lm = lneg
            hi, lo = _hi_lo(lm)
            between = _dot(later, hi) + _dot(later, lo) + run_row
            a_w = jnp.exp(lneg + z + between)
            if diagonal:
                a_w = jnp.where(msk, a_w, 0.0)
            acc = acc + _dot(vt_ref[0, 0, kt], a_w.astype(BF16))
            return run_row + jnp.sum(lm, axis=0, keepdims=True), acc

        run_row, acc = tile(qi, jnp.zeros((1, nh * tq), F32), jnp.zeros((LANES, nh * tq), F32), True)

        def alive(c):
            return (c[0] >= 0) & (jnp.max(c[1]) > SB_DEAD)

        def step(c):
            r, a = tile(c[0], c[1], c[2], False)
            return c[0] - 1, r, a

        acc = lax.while_loop(alive, step, (qi - 1, run_row, acc))[2]
        if nh == 2:
            o_t = jnp.where(row < HEAD_DIM, acc[:, 0:tq], acc[:, tq:2 * tq])
        else:
            o_t = jnp.where(row < HEAD_DIM, acc, 0.0)
        o_ref[0] = o_t.T.astype(BF16)

    @pl.when(pair * 2 + 1 < n_heads)
    def _():
        run(2)

    @pl.when(pair * 2 + 1 >= n_heads)
    def _():
        run(1)


def _sb_attention(sb, vt):
    b, s, _ = sb.shape
    tq = KEY_TILE
    kern = functools.partial(_sb_kernel, tq=tq, n_heads=SB_HEADS)
    return pl.pallas_call(
        kern,
        grid=(b, PAIR_SLABS, s // tq),
        in_specs=[pl.BlockSpec((1, tq, LANES), lambda i, p, j: (i, j, p)),
                  pl.BlockSpec((1, s, LANES), lambda i, p, j: (i, 0, PAIR_SLABS + p)),
                  pl.BlockSpec((1, 1, s // KEY_TILE, LANES, KEY_TILE), lambda i, p, j: (i, 2 + p, 0, 0, 0))],
        out_specs=pl.BlockSpec((1, tq, LANES), lambda i, p, j: (i, j, p)),
        out_shape=jax.ShapeDtypeStruct((b, s, PAIR_D), BF16),
        compiler_params=_cparams(("arbitrary", "arbitrary", "arbitrary")),
        name="sb_attention",
    )(sb, sb, vt)


def _mla_kernel(qn_ref, qr_ref, kn_ref, kpe_ref, vt_ref, o_ref, s_buf, *, tq, n_heads):
    pair = pl.program_id(1)
    t0 = pl.program_id(2) * tq
    per_sweep = SWEEP // KEY_TILE
    q = jnp.concatenate([qn_ref[0], qr_ref[0]], axis=1)
    lane2 = _iota((1, 2 * LANES), 1)
    lane_head = jnp.where(lane2 < LANES, lane2 // MLA_NOPE, (lane2 - LANES) // MLA_ROPE)
    t_row1 = t0 + _iota((1, tq), 1)
    row = _iota((LANES, 1), 0)
    scale = (MLA_NOPE + MLA_ROPE) ** -0.5
    ones_rows = jnp.ones((ONES_ROWS, KEY_TILE), BF16)
    n_full = t0 // SWEEP
    n_it = (t0 + tq + SWEEP - 1) // SWEEP

    def run(nh):
        m = nh * tq
        qs = jnp.concatenate([jnp.where(lane_head == a, q, jnp.zeros((), BF16))
                              for a in range(nh)], axis=0)
        t_row = _lanes([t_row1] * nh)

        def scores(it, m8, diagonal):
            k0 = pl.multiple_of(it * SWEEP, SWEEP)
            k = jnp.concatenate([kn_ref[0, pl.ds(k0, SWEEP), :], kpe_ref[0, pl.ds(k0, SWEEP), :]], axis=1)
            sc = _dot_nt(k, qs) * (scale * LOG2E)
            if diagonal:
                sc = jnp.where((k0 + _iota((SWEEP, 1), 0)) <= t_row, sc, -jnp.inf)
            s_buf[it, :, 0:m] = sc
            return jnp.maximum(m8, jnp.max(sc.reshape(-1, 8, m), axis=0))

        m8 = _sweep_loop(0, n_full, lambda it, c: scores(it, c, False), jnp.full((8, m), M_FLOOR, F32))
        m8 = lax.fori_loop(n_full, n_it, lambda it, c: scores(it, c, True), m8)
        m_f = jnp.max(m8, axis=0, keepdims=True)

        def values(it, acc):
            pb = jnp.exp2(s_buf[it, :, 0:m] - m_f).astype(BF16)
            for h in range(per_sweep):
                v_aug = jnp.concatenate([vt_ref[0, 0, it * per_sweep + h], ones_rows], axis=0)
                acc = acc + _dot(v_aug, pb[h * KEY_TILE:(h + 1) * KEY_TILE])
            return acc

        acc = _sweep_loop(0, n_it, values, jnp.zeros((LANES + ONES_ROWS, m), F32))
        o = acc[0:LANES] / acc[LANES:LANES + 1]
        if nh == 2:
            o_t = jnp.where(row < MLA_V, o[:, 0:tq], o[:, tq:2 * tq])
        else:
            o_t = jnp.where(row < MLA_V, o, 0.0)
        o_ref[0] = o_t.T.astype(BF16)

    @pl.when(pair * 2 + 1 < n_heads)
    def _():
        run(2)

    @pl.when(pair * 2 + 1 >= n_heads)
    def _():
        run(1)


def _mla_attention(mla, vt, tq=SWEEP):
    b, s, _ = mla.shape
    kern = functools.partial(_mla_kernel, tq=tq, n_heads=MLA_HEADS)
    ps = PAIR_SLABS
    return pl.pallas_call(
        kern,
        grid=(b, ps, s // tq),
        in_specs=[pl.BlockSpec((1, tq, LANES), lambda i, p, j: (i, j, p)),
                  pl.BlockSpec((1, tq, LANES), lambda i, p, j: (i, j, ps + p)),
                  pl.BlockSpec((1, s, LANES), lambda i, p, j: (i, 0, 2 * ps + p)),
                  pl.BlockSpec((1, s, LANES), lambda i, p, j: (i, 0, 3 * ps)),
                  pl.BlockSpec((1, 1, s // KEY_TILE, LANES, KEY_TILE),
                               lambda i, p, j: (i, 2 + ps + p, 0, 0, 0))],
        out_specs=pl.BlockSpec((1, tq, LANES), lambda i, p, j: (i, j, p)),
        out_shape=jax.ShapeDtypeStruct((b, s, PAIR_D), BF16),
        scratch_shapes=[pltpu.VMEM((s // SWEEP, SWEEP, 2 * tq), F32)],
        compiler_params=_cparams(("arbitrary", "arbitrary", "arbitrary")),
        name="mla_attention",
    )(mla, mla, mla, mla, vt)


def _outproj_kernel(oa_ref, ob_ref, oc_ref, z_ref, x_ref, mod_ref, w_ref, fg_ref, o_ref, *, final):
    mix = jnp.concatenate([oa_ref[0], ob_ref[0], oc_ref[0]], axis=1).astype(F32) * jax.nn.silu(z_ref[0])
    y = _dot(mix.astype(BF16), w_ref[...])
    xn = x_ref[0] + mod_ref[0, 2:3, :] * y
    if final:
        ms = jnp.mean(xn * xn, axis=-1, keepdims=True)
        xn = xn * lax.rsqrt(ms + EPS) * fg_ref[...]
    o_ref[0] = xn


def _out_projection(oa, ob, oc, z, x, mod3, w_out_p, final_g, final, ts=256):
    b, s, d = x.shape
    row = lambda n: pl.BlockSpec((1, ts, n), lambda i, j: (i, j, 0))
    full = lambda a: pl.BlockSpec(a.shape, lambda i, j: (0,) * a.ndim)
    return pl.pallas_call(
        functools.partial(_outproj_kernel, final=final),
        grid=(b, s // ts),
        in_specs=[row(NSA_D), row(PAIR_D), row(PAIR_D), row(3 * PAIR_D), row(d),
                  pl.BlockSpec((1, 3, d), lambda i, j: (i, 0, 0)), full(w_out_p), full(final_g)],
        out_specs=row(d),
        out_shape=jax.ShapeDtypeStruct((b, s, d), F32),
        compiler_params=_cparams(("arbitrary", "arbitrary")),
        name="out_projection",
    )(oa, ob, oc, z, x, mod3, w_out_p, final_g)


def _overlap_matrix_t(s):
    n_cmp = (s - CMP_LEN) // CMP_STRIDE + 1
    n_sel = s // SEL_BLOCK
    c_start = np.arange(s // CMP_STRIDE) * CMP_STRIDE
    s_start = np.arange(n_sel) * SEL_BLOCK
    ov = ((c_start[None, :] < s_start[:, None] + SEL_BLOCK)
          & (c_start[None, :] + CMP_LEN > s_start[:, None])
          & (np.arange(s // CMP_STRIDE)[None, :] < n_cmp))
    return jnp.asarray(ov.astype(np.float32), dtype=BF16)


def _mixer_layer(x, mod3, tabs, overlap_t, norm_g, w_in, pos_k, pos_v, ck_w1, ck_w2, cv_w1, cv_w2,
                 q_norm, w_uq, kv_norm, w_ukv, w_out, final_g, final):
    b, s, d = x.shape
    w_in_p = _take_cols(w_in, _in_cols()).astype(BF16)
    w_uq_p = jnp.pad(_take_cols(w_uq, _uq_cols()), ((0, Q_RANK_PAD - MLA_Q_RANK), (0, 0))).astype(BF16)
    w_ukv_p = _take_cols(w_ukv, _ukv_cols()).astype(BF16)
    q_norm_p = jnp.pad(q_norm, (0, Q_RANK_PAD - MLA_Q_RANK)).reshape(1, Q_RANK_PAD)
    w_out_p = _take_cols(w_out.T, _out_rows()).T.astype(BF16)

    nq, nk, cmp_tok, gate_t, z, sb, mla, vt = _in_projection(
        x, mod3, norm_g.reshape(1, d), w_in_p, q_norm_p, w_uq_p, kv_norm.reshape(1, -1), w_ukv_p, tabs)

    chunk_w = CMP_STRIDE * LANES
    kc_chunks = cmp_tok[:, :, 0:LANES].reshape(b, s // CMP_STRIDE, chunk_w)
    vc_chunks = cmp_tok[:, :, LANES:].reshape(b, s // CMP_STRIDE, chunk_w)
    pos4 = jnp.concatenate([_cmp_pos(pos_k), _cmp_pos(pos_v)], axis=0)
    kc, vct = _nsa_compress(kc_chunks, vc_chunks, pos4,
                            _cmp_weights(ck_w1, ck_w2), _cmp_weights(cv_w1, cv_w2))

    o_a = _nsa_attention(nq, gate_t, kc, vct, nk, vt, overlap_t)
    o_b = _sb_attention(sb, vt)
    o_c = _mla_attention(mla, vt)
    return _out_projection(o_a, o_b, o_c, z, x, mod3, w_out_p, final_g.reshape(1, d), final)


def kernel(x, c, positions, ada_w, ada_b, norm_g, w_in, nsa_pos_k, nsa_pos_v, nsa_ck_w1, nsa_ck_w2,
           nsa_cv_w1, nsa_cv_w2, mla_q_norm, mla_w_uq, mla_kv_norm, mla_w_ukv, w_out, final_norm):
    b, s, d = x.shape
    depth = w_in.shape[0]
    assert s % SWEEP == 0 and s >= WINDOW + 2 * KEY_TILE
    mod = _adaln_mod(c, ada_w, ada_b).reshape(depth, b, 3, d)
    tabs = _rope_tables(positions)
    overlap_t = _overlap_matrix_t(s)
    for l in range(depth):
        x = _mixer_layer(x, mod[l], tabs, overlap_t, norm_g[l], w_in[l], nsa_pos_k[l], nsa_pos_v[l],
                         nsa_ck_w1[l], nsa_ck_w2[l], nsa_cv_w1[l], nsa_cv_w2[l],
                         mla_q_norm[l], mla_w_uq[l], mla_kv_norm[l], mla_w_ukv[l], w_out[l],
                         final_norm, final=(l == depth - 1))
    return x
```

```python
import functools

import numpy as np
import jax
import jax.numpy as jnp
from jax import lax
from jax.experimental import pallas as pl
from jax.experimental.pallas import tpu as pltpu

F32 = jnp.float32
BF16 = jnp.bfloat16

LANES = 128
HEAD_DIM = 64
NSA_HEADS = 6
NSA_GROUPS = 2
NSA_HPG = NSA_HEADS // NSA_GROUPS
SB_HEADS = 5
MLA_HEADS = 5
MLA_NOPE = 64
MLA_ROPE = 32
MLA_V = 64
MLA_Q_RANK = 192
MLA_KV_RANK = 128
CMP_LEN = 32
CMP_STRIDE = 16
CMP_HIDDEN = 128
SEL_BLOCK = 64
SEL_TOPK = 16
WINDOW = 512
ROPE_THETA = 10000.0
EPS = 1e-6
SEL_FORCE = 1e9
M_FLOOR = -1e30
SB_DEAD = -110.0
LOG2E = 1.4426950408889634
NEG_MASK = -1e30
ONES_ROWS = 16

NSA_D = NSA_HEADS * HEAD_DIM
PAIR_SLABS = 3
PAIR_D = PAIR_SLABS * LANES
Q_RANK_PAD = 256

KEY_TILE = 256
SWEEP = 2 * KEY_TILE
VT_SLABS = 8

C_NQ = 0
C_NKV = C_NQ + NSA_D
C_CMP = C_NKV + 4 * LANES
C_GATE = C_CMP + 2 * LANES
C_Z = C_GATE + LANES
C_SB = C_Z + 3 * PAIR_D
C_CQ = C_SB + 3 * PAIR_D
C_CKV = C_CQ + Q_RANK_PAD
C_KR = C_CKV + LANES
D_IN_PAD = C_KR + LANES

VMEM_LIMIT = 52 * 1024 * 1024


def _cparams(sem):
    return pltpu.CompilerParams(dimension_semantics=sem, vmem_limit_bytes=VMEM_LIMIT)


def _dot(a, b):
    return jnp.dot(a, b, preferred_element_type=F32)


def _dot_nt(a, b):
    return lax.dot_general(a, b, (((1,), (1,)), ((), ())), preferred_element_type=F32)


def _iota(shape, dim):
    return lax.broadcasted_iota(jnp.int32, shape, dim)


def _rope(x, cos, sin_signed, width):
    half = width // 2
    lane = _iota(x.shape, 1)
    first = (lane % width) < half
    rot = jnp.where(first, pltpu.roll(x, LANES - half, 1), pltpu.roll(x, half, 1))
    return x * cos + rot * sin_signed


def _hi_lo(x):
    hi = x.astype(BF16)
    return hi, (x - hi.astype(F32)).astype(BF16)


def _lanes(parts):
    return parts[0] if len(parts) == 1 else jnp.concatenate(parts, axis=1)


def _sweep_loop(lo, hi, step, carry):
    pairs = (hi - lo) // 2
    carry = lax.fori_loop(0, pairs, lambda i, c: step(lo + 2 * i + 1, step(lo + 2 * i, c)), carry)
    return lax.fori_loop(lo + 2 * pairs, hi, step, carry)


def _in_cols():
    widths = (NSA_D, 128, 128, 128, 128, 128, 128, NSA_HEADS * 3, NSA_D,
              320, 320, 320, 320, MLA_Q_RANK, MLA_KV_RANK, MLA_ROPE, 320)
    off = np.concatenate([[0], np.cumsum(widths)])
    (o_nq, o_kc, o_vc, o_ks, o_vs, o_kw, o_vw, o_gate, o_nz,
     o_sq, o_sk, o_sv, o_sz, o_cq, o_ckv, o_kr, o_mz) = [int(v) for v in off[:-1]]
    cols = -np.ones((D_IN_PAD,), np.int64)

    def put(dst, src, n):
        cols[dst:dst + n] = src + np.arange(n)

    def put_nsa(dst, src):
        for i in range(NSA_HPG):
            for g in range(NSA_GROUPS):
                put(dst + i * LANES + g * HEAD_DIM, src + (g * NSA_HPG + i) * HEAD_DIM, HEAD_DIM)

    put_nsa(C_NQ, o_nq)
    put(C_NKV, o_ks, 128)
    put(C_NKV + 128, o_vs, 128)
    put(C_NKV + 256, o_kw, 128)
    put(C_NKV + 384, o_vw, 128)
    put(C_CMP, o_kc, 128)
    put(C_CMP + 128, o_vc, 128)
    put(C_GATE, o_gate, NSA_HEADS * 3)
    put_nsa(C_Z, o_nz)
    put(C_Z + PAIR_D, o_sz, 320)
    put(C_Z + 2 * PAIR_D, o_mz, 320)
    put(C_SB, o_sq, 320)
    put(C_SB + PAIR_D, o_sk, 320)
    put(C_SB + 2 * PAIR_D, o_sv, 320)
    put(C_CQ, o_cq, MLA_Q_RANK)
    put(C_CKV, o_ckv, MLA_KV_RANK)
    put(C_KR, o_kr, MLA_ROPE)
    put(C_KR + MLA_ROPE, o_kr, MLA_ROPE)
    return cols


def _take_cols(w, cols):
    cols = np.asarray(cols)
    g = jnp.take(w, jnp.asarray(np.maximum(cols, 0)), axis=1)
    return jnp.where(jnp.asarray(cols >= 0)[None, :], g, 0.0)


def _uq_cols():
    cols = -np.ones((2 * PAIR_D,), np.int64)
    for h in range(MLA_HEADS):
        p, a = divmod(h, 2)
        base = h * (MLA_NOPE + MLA_ROPE)
        cols[p * LANES + a * MLA_NOPE: p * LANES + (a + 1) * MLA_NOPE] = base + np.arange(MLA_NOPE)
        d = PAIR_D + p * LANES + a * MLA_ROPE
        cols[d: d + MLA_ROPE] = base + MLA_NOPE + np.arange(MLA_ROPE)
    return cols


def _ukv_cols():
    cols = -np.ones((2 * PAIR_D,), np.int64)
    for h in range(MLA_HEADS):
        base = h * (MLA_NOPE + MLA_V)
        cols[h * MLA_NOPE: (h + 1) * MLA_NOPE] = base + np.arange(MLA_NOPE)
        cols[PAIR_D + h * MLA_V: PAIR_D + (h + 1) * MLA_V] = base + MLA_NOPE + np.arange(MLA_V)
    return cols


def _out_rows():
    rows = -np.ones((3 * PAIR_D,), np.int64)
    for i in range(NSA_HPG):
        for g in range(NSA_GROUPS):
            d = i * LANES + g * HEAD_DIM
            rows[d:d + HEAD_DIM] = (g * NSA_HPG + i) * HEAD_DIM + np.arange(HEAD_DIM)
    rows[PAIR_D:PAIR_D + 320] = NSA_D + np.arange(320)
    rows[2 * PAIR_D:2 * PAIR_D + 320] = NSA_D + 320 + np.arange(320)
    return rows


def _cmp_weights(w1, w2):
    half = CMP_LEN // 2
    w1r = w1.reshape(CMP_LEN, HEAD_DIM, CMP_HIDDEN)
    z = jnp.zeros((half, HEAD_DIM, CMP_HIDDEN), w1.dtype)

    def build(part):
        g0 = jnp.concatenate([part, z], axis=1)
        g1 = jnp.concatenate([z, part], axis=1)
        return jnp.concatenate([g0.reshape(half * LANES, CMP_HIDDEN),
                                g1.reshape(half * LANES, CMP_HIDDEN)], axis=1)

    zz = jnp.zeros_like(w2)
    w2bd = jnp.concatenate([jnp.concatenate([w2, zz], axis=1),
                            jnp.concatenate([zz, w2], axis=1)], axis=0)
    return build(w1r[:half]).astype(BF16), build(w1r[half:]).astype(BF16), w2bd.astype(BF16)


def _cmp_pos(pos):
    half = CMP_LEN // 2
    tiled = jnp.concatenate([pos, pos], axis=1)
    return jnp.stack([tiled[:half].reshape(half * LANES), tiled[half:].reshape(half * LANES)])


def _mod_kernel(c_ref, w_ref, b_ref, o_ref):
    a = jax.nn.silu(c_ref[...])
    o_ref[0] = _dot(a, w_ref[0]) + b_ref[0]


def _adaln_mod(c, ada_w, ada_b):
    depth, d, n = ada_w.shape
    b = c.shape[0]
    tn = 1024
    return pl.pallas_call(
        _mod_kernel,
        grid=(depth, n // tn),
        in_specs=[pl.BlockSpec((b, d), lambda l, j: (0, 0)),
                  pl.BlockSpec((1, d, tn), lambda l, j: (l, 0, j)),
                  pl.BlockSpec((1, 1, tn), lambda l, j: (l, 0, j))],
        out_specs=pl.BlockSpec((1, b, tn), lambda l, j: (l, 0, j)),
        out_shape=jax.ShapeDtypeStruct((depth, b, n), F32),
        compiler_params=_cparams(("arbitrary", "arbitrary")),
        name="adaln_mod",
    )(c, ada_w, ada_b.reshape(depth, 1, n))


def _rope_kernel(pos_ref, inv_ref, c64_ref, s64_ref, c32_ref, s32_ref):
    pos = pos_ref[0].astype(F32)
    lane = _iota((1, LANES), 1)
    ang = pos * inv_ref[0:1, :]
    n64, n32 = HEAD_DIM // 2, MLA_ROPE // 2

    def tile_lanes(x, first, period):
        y = jnp.where((lane >= first) & (lane < first + period), x, 0.0)
        if first:
            y = pltpu.roll(y, LANES - first, 1)
        width = period
        while width < LANES:
            y = y + pltpu.roll(y, width, 1)
            width *= 2
        return y

    cos, sin = jnp.cos(ang), jnp.sin(ang)
    c64_ref[0] = tile_lanes(cos, 0, n64)
    s64 = tile_lanes(sin, 0, n64)
    s64_ref[0] = jnp.where((lane % HEAD_DIM) < n64, -s64, s64)
    c32_ref[0] = tile_lanes(cos, n64, n32)
    s32 = tile_lanes(sin, n64, n32)
    s32_ref[0] = jnp.where((lane % MLA_ROPE) < n32, -s32, s32)


def _rope_tables(positions):
    b, s = positions.shape
    ts = 512

    def inv(half):
        return ROPE_THETA ** (-jnp.arange(half, dtype=F32) / half)

    packed = jnp.concatenate([inv(HEAD_DIM // 2), inv(MLA_ROPE // 2)])
    inv_tab = jnp.zeros((8, LANES), F32).at[0, 0:packed.shape[0]].set(packed)
    tab = jax.ShapeDtypeStruct((b, s, LANES), F32)
    spec = pl.BlockSpec((1, ts, LANES), lambda i, j: (i, j, 0))
    return pl.pallas_call(
        _rope_kernel,
        grid=(b, s // ts),
        in_specs=[pl.BlockSpec((1, ts, 1), lambda i, j: (i, j, 0)),
                  pl.BlockSpec((8, LANES), lambda i, j: (0, 0))],
        out_specs=[spec] * 4,
        out_shape=[tab] * 4,
        compiler_params=_cparams(("arbitrary", "arbitrary")),
        name="rope_tables",
    )(positions.reshape(b, s, 1), inv_tab)


def _inproj_kernel(x_ref, mod_ref, g_ref, w_ref, qn_ref, wuq_ref, kvn_ref, wukv_ref,
                   c64_ref, s64_ref, c32_ref, s32_ref,
                   nq_ref, nk_ref, cmp_ref, gatet_ref, z_ref, sb_ref, mla_ref, vt_ref):
    x = x_ref[0]
    ms = jnp.mean(x * x, axis=-1, keepdims=True)
    y = x * lax.rsqrt(ms + EPS) * g_ref[...]
    h = y * (1.0 + mod_ref[0, 1:2, :]) + mod_ref[0, 0:1, :]
    hb = h.astype(BF16)

    def proj(c0, n):
        return _dot(hb, w_ref[:, c0:c0 + n])

    def slab(val, c0, j):
        return val[:, c0 + j * LANES:c0 + (j + 1) * LANES]

    def put_vt(slab_id, val):
        for sub in range(val.shape[0] // KEY_TILE):
            vt_ref[0, slab_id, sub] = val[sub * KEY_TILE:(sub + 1) * KEY_TILE].T.astype(BF16)

    c64, s64 = c64_ref[0], s64_ref[0]
    c32, s32 = c32_ref[0], s32_ref[0]
    q_scale = HEAD_DIM ** -0.5

    nsa = proj(C_NQ, C_Z - C_NQ)
    for i in range(NSA_HPG):
        q = _rope(slab(nsa, C_NQ, i), c64, s64, HEAD_DIM)
        nq_ref[0, :, i * LANES:(i + 1) * LANES] = (q * q_scale).astype(BF16)
    nk_ref[0, :, 0:LANES] = _rope(slab(nsa, C_NKV, 0), c64, s64, HEAD_DIM).astype(BF16)
    put_vt(0, slab(nsa, C_NKV, 1))
    nk_ref[0, :, LANES:2 * LANES] = _rope(slab(nsa, C_NKV, 2), c64, s64, HEAD_DIM).astype(BF16)
    put_vt(1, slab(nsa, C_NKV, 3))
    cmp_ref[0, :, 0:LANES] = _rope(slab(nsa, C_CMP, 0), c64, s64, HEAD_DIM)
    cmp_ref[0, :, LANES:2 * LANES] = slab(nsa, C_CMP, 1)
    gatet_ref[0] = jax.nn.sigmoid(slab(nsa, C_GATE, 0)).T
    z_ref[0] = proj(C_Z, 3 * PAIR_D)
    sbp = proj(C_SB, 3 * PAIR_D)
    sb_ref[0, :, 0:PAIR_D] = (sbp[:, 0:PAIR_D] * q_scale).astype(BF16)
    sb_ref[0, :, PAIR_D:2 * PAIR_D] = sbp[:, PAIR_D:2 * PAIR_D].astype(BF16)
    for p in range(PAIR_SLABS):
        put_vt(2 + p, slab(sbp, 2 * PAIR_D, p))

    lat = proj(C_CQ, D_IN_PAD - C_CQ)
    cq = lat[:, 0:Q_RANK_PAD]
    cqn = cq * lax.rsqrt(jnp.sum(cq * cq, axis=-1, keepdims=True) / MLA_Q_RANK + EPS) * qn_ref[...]
    q2 = _dot(cqn.astype(BF16), wuq_ref[...])
    mla_ref[0, :, 0:PAIR_D] = q2[:, 0:PAIR_D].astype(BF16)
    for p in range(PAIR_SLABS):
        r = _rope(q2[:, PAIR_D + p * LANES:PAIR_D + (p + 1) * LANES], c32, s32, MLA_ROPE)
        mla_ref[0, :, PAIR_D + p * LANES:PAIR_D + (p + 1) * LANES] = r.astype(BF16)
    ckv = lat[:, Q_RANK_PAD:Q_RANK_PAD + LANES]
    ckvn = ckv * lax.rsqrt(jnp.mean(ckv * ckv, axis=-1, keepdims=True) + EPS) * kvn_ref[...]
    kv2 = _dot(ckvn.astype(BF16), wukv_ref[...])
    mla_ref[0, :, 2 * PAIR_D:3 * PAIR_D] = kv2[:, 0:PAIR_D].astype(BF16)
    for p in range(PAIR_SLABS):
        put_vt(2 + PAIR_SLABS + p, slab(kv2, PAIR_D, p))
    kr = _rope(lat[:, Q_RANK_PAD + LANES:Q_RANK_PAD + 2 * LANES], c32, s32, MLA_ROPE)
    mla_ref[0, :, 3 * PAIR_D:3 * PAIR_D + LANES] = kr.astype(BF16)


def _in_projection(x, mod3, norm_g, w_in_p, q_norm_p, w_uq_p, kv_norm, w_ukv_p, tabs):
    b, s, d = x.shape
    ts = 2 * KEY_TILE
    row = lambda n: pl.BlockSpec((1, ts, n), lambda i, j: (i, j, 0))
    full = lambda a: pl.BlockSpec(a.shape, lambda i, j: (0,) * a.ndim, pipeline_mode=pl.Buffered(1))
    outs = [(NSA_D, BF16), (2 * LANES, BF16), (2 * LANES, F32), None,
            (3 * PAIR_D, F32), (2 * PAIR_D, BF16), (3 * PAIR_D + LANES, BF16), None]
    out_specs = [row(o[0]) if o else None for o in outs]
    out_shape = [jax.ShapeDtypeStruct((b, s, o[0]), o[1]) if o else None for o in outs]
    out_specs[3] = pl.BlockSpec((1, LANES, ts), lambda i, j: (i, 0, j))
    out_shape[3] = jax.ShapeDtypeStruct((b, LANES, s), F32)
    out_specs[7] = pl.BlockSpec((1, VT_SLABS, ts // KEY_TILE, LANES, KEY_TILE), lambda i, j: (i, 0, j, 0, 0))
    out_shape[7] = jax.ShapeDtypeStruct((b, VT_SLABS, s // KEY_TILE, LANES, KEY_TILE), BF16)
    return pl.pallas_call(
        _inproj_kernel,
        grid=(b, s // ts),
        in_specs=[row(d), pl.BlockSpec((1, 3, d), lambda i, j: (i, 0, 0)), full(norm_g),
                  full(w_in_p), full(q_norm_p), full(w_uq_p), full(kv_norm), full(w_ukv_p),
                  row(LANES), row(LANES), row(LANES), row(LANES)],
        out_specs=out_specs,
        out_shape=out_shape,
        compiler_params=_cparams(("arbitrary", "arbitrary")),
        name="in_projection",
    )(x, mod3, norm_g, w_in_p, q_norm_p, w_uq_p, kv_norm, w_ukv_p, *tabs)


def _cmp_kernel(kc_ref, vc_ref, pos_ref, wkt_ref, wkb_ref, wk2_ref, wvt_ref, wvb_ref, wv2_ref,
                kc_out, vct_out):
    def compress(chunks, pos_row, wt_ref, wb_ref, w2_ref):
        a = _dot((chunks + pos_ref[pos_row:pos_row + 1, :]).astype(BF16), wt_ref[...])
        bm = _dot((chunks + pos_ref[pos_row + 1:pos_row + 2, :]).astype(BF16), wb_ref[...])
        n = chunks.shape[0]
        pre = a + pltpu.roll(bm, n - 1, 0)
        return _dot(jax.nn.silu(pre).astype(BF16), w2_ref[...])

    kc_out[0] = compress(kc_ref[0], 0, wkt_ref, wkb_ref, wk2_ref).astype(BF16)
    vct_out[0] = compress(vc_ref[0], 2, wvt_ref, wvb_ref, wv2_ref).T.astype(BF16)


def _nsa_compress(kc_chunks, vc_chunks, pos4, wk, wv):
    b, nc, cw = kc_chunks.shape
    full = lambda a: pl.BlockSpec(a.shape, lambda i: (0,) * a.ndim)
    chunk = pl.BlockSpec((1, nc, cw), lambda i: (i, 0, 0))
    return pl.pallas_call(
        _cmp_kernel,
        grid=(b,),
        in_specs=[chunk, chunk, full(pos4)] + [full(w) for w in wk] + [full(w) for w in wv],
        out_specs=[pl.BlockSpec((1, nc, LANES), lambda i: (i, 0, 0)),
                   pl.BlockSpec((1, LANES, nc), lambda i: (i, 0, 0))],
        out_shape=[jax.ShapeDtypeStruct((b, nc, LANES), BF16),
                   jax.ShapeDtypeStruct((b, LANES, nc), BF16)],
        compiler_params=_cparams(("arbitrary",)),
        name="nsa_compress",
    )(kc_chunks, vc_chunks, pos4, *wk, *wv)


def _nsa_kernel(q_ref, gt_ref, kc_ref, vct_ref, k_ref, vt_ref, ovt_ref, oh_ref, o_ref, s_buf,
                *, tq, top):
    t0 = pl.program_id(1) * tq
    nc = kc_ref.shape[1]
    ns = ovt_ref.shape[0]
    per_sweep = SWEEP // KEY_TILE
    win_tiles = -(-(WINDOW + tq + max(KEY_TILE - tq, 0)) // KEY_TILE)
    ones_rows = jnp.ones((ONES_ROWS, KEY_TILE), BF16)
    m = NSA_HPG * tq
    q = q_ref[0]
    gt = gt_ref[0]
    lane = _iota((1, LANES), 1)
    t_row = t0 + _iota((1, tq), 1)

    def heads(a):
        return _lanes([a] * NSA_HPG)

    t_row3 = heads(t_row)
    vis_bias = jnp.where((_iota((nc, 1), 0) * CMP_STRIDE + (CMP_LEN - 1)) <= t_row, 0.0, -jnp.inf)
    j_col = _iota((ns, 1), 0)
    cur = t_row // SEL_BLOCK
    forced = (j_col == 0) | (j_col == cur) | (j_col == cur - 1)
    valid = j_col * SEL_BLOCK <= t_row
    n_full = t0 // SWEEP
    n_it = (t0 + tq + SWEEP - 1) // SWEEP
    wt = jnp.maximum(t0 - WINDOW, 0) // KEY_TILE
    w0 = pl.multiple_of(wt * KEY_TILE, KEY_TILE)
    wlen = win_tiles * KEY_TILE
    tokw = w0 + _iota((wlen, 1), 0)
    win_bias = jnp.where((tokw <= t_row) & (tokw > t_row - WINDOW), 0.0, -jnp.inf)

    def gate_row(g, br):
        return _lanes([gt[(g * NSA_HPG + i) * 3 + br:(g * NSA_HPG + i) * 3 + br + 1, :]
                       for i in range(NSA_HPG)])

    def value_rows(g, slab_id, kt):
        return jnp.concatenate([vt_ref[0, slab_id, kt, g * HEAD_DIM:(g + 1) * HEAD_DIM, :], ones_rows],
                               axis=0)

    def prologue(g):
        qs = jnp.concatenate([jnp.where((lane // HEAD_DIM) == g, q[:, i * LANES:(i + 1) * LANES],
                                        jnp.zeros((), BF16)) for i in range(NSA_HPG)], axis=0)

        sw = _dot_nt(k_ref[0, pl.ds(w0, wlen), LANES:2 * LANES], qs) + heads(win_bias)
        pwb = jnp.exp(sw - jnp.max(sw, axis=0, keepdims=True)).astype(BF16)
        acc_w = jnp.zeros((HEAD_DIM + ONES_ROWS, m), F32)
        for h in range(win_tiles):
            acc_w = acc_w + _dot(value_rows(g, 1, wt + h), pwb[h * KEY_TILE:(h + 1) * KEY_TILE])
        o_w = acc_w[0:HEAD_DIM] / acc_w[HEAD_DIM:HEAD_DIM + 1]

        s = _dot_nt(kc_ref[0], qs) + heads(vis_bias)
        mx = jnp.maximum(jnp.max(s, axis=0, keepdims=True), M_FLOOR)
        p = jnp.exp(s - mx)
        l = jnp.sum(p, axis=0, keepdims=True)
        pc = p * jnp.where(l > 0.0, 1.0 / l, 0.0)
        o_c = _dot(vct_ref[0, g * HEAD_DIM:(g + 1) * HEAD_DIM, :], pc.astype(BF16))

        hi, lo = _hi_lo(pc[:, 0:tq] + pc[:, tq:2 * tq] + pc[:, 2 * tq:3 * tq])
        imp_sel = _dot(ovt_ref[...], hi) + _dot(ovt_ref[...], lo)
        rank = _rank(jnp.where(valid, jnp.where(forced, SEL_FORCE, imp_sel), -SEL_FORCE))

        member = jnp.concatenate([jnp.where(rank < top, 0.0, NEG_MASK), jnp.zeros((LANES - ns, tq), F32)],
                                 axis=0).T.astype(BF16)
        qs_aug = jnp.concatenate([qs, jnp.concatenate([member] * NSA_HPG, axis=0)], axis=1)
        return qs_aug, gate_row(g, 0) * o_c + gate_row(g, 2) * o_w

    def scores(g, qs_aug, it, m8, diagonal):
        k0 = pl.multiple_of(it * SWEEP, SWEEP)
        k_aug = jnp.concatenate([k_ref[0, pl.ds(k0, SWEEP), 0:LANES], oh_ref[pl.ds(k0, SWEEP), :]], axis=1)
        sc = _dot_nt(k_aug, qs_aug) * LOG2E
        if diagonal:
            sc = jnp.where((k0 + _iota((SWEEP, 1), 0)) <= t_row3, sc, -jnp.inf)
        s_buf[g, it] = sc
        return jnp.maximum(m8, jnp.max(sc.reshape(-1, 8, m), axis=0))

    def values(g, m_s, it, acc):
        pb = jnp.exp2(s_buf[g, it] - m_s).astype(BF16)
        for h in range(per_sweep):
            acc = acc + _dot(value_rows(g, 0, it * per_sweep + h), pb[h * KEY_TILE:(h + 1) * KEY_TILE])
        return acc

    pro = [prologue(g) for g in range(NSA_GROUPS)]
    qa = [pro[g][0] for g in range(NSA_GROUPS)]
    m8_init = jnp.full((8, m), M_FLOOR, F32)
    acc_init = jnp.zeros((HEAD_DIM + ONES_ROWS, m), F32)

    m8 = _sweep_loop(0, n_full, lambda it, c: scores(0, qa[0], it, c, False), m8_init)
    m8 = lax.fori_loop(n_full, n_it, lambda it, c: scores(0, qa[0], it, c, True), m8)
    m_s0 = jnp.max(m8, axis=0, keepdims=True)

    def fused(diagonal):
        return lambda it, c: (values(0, m_s0, it, c[0]), scores(1, qa[1], it, c[1], diagonal))

    acc0, m8 = _sweep_loop(0, n_full, fused(False), (acc_init, m8_init))
    acc0, m8 = lax.fori_loop(n_full, n_it, fused(True), (acc0, m8))
    m_s1 = jnp.max(m8, axis=0, keepdims=True)
    acc1 = _sweep_loop(0, n_it, lambda it, c: values(1, m_s1, it, c), acc_init)

    mix = [pro[g][1] + gate_row(g, 1) * (acc[0:HEAD_DIM] / acc[HEAD_DIM:HEAD_DIM + 1])
           for g, acc in enumerate((acc0, acc1))]
    for i in range(NSA_HPG):
        slab_t = jnp.concatenate([mix[g][:, i * tq:(i + 1) * tq] for g in range(NSA_GROUPS)], axis=0)
        o_ref[0, :, i * LANES:(i + 1) * LANES] = slab_t.T.astype(BF16)


def _rank(x):
    ns, tq = x.shape
    sub = _iota((8, 1), 0)
    parts = [x[r:r + 8] for r in range(0, ns, 8)]
    counts = [jnp.zeros((8, tq), F32) for _ in parts]
    for jp in range(ns):
        r = x[jp:jp + 1, :]
        for n, part in enumerate(parts):
            if 8 * n > jp:
                beats = r >= part
            elif 8 * n + 7 < jp:
                beats = r > part
            else:
                beats = (r > part) | ((r == part) & (sub + 8 * n > jp))
            counts[n] = counts[n] + jnp.where(beats, 1.0, 0.0)
    return jnp.concatenate(counts, axis=0)


def _nsa_attention(nq, gate_t, kc, vct, nk, vt, overlap_t, tq=256):
    b, s, _ = nq.shape
    nc = kc.shape[1]
    ns = overlap_t.shape[0]
    assert ns <= LANES
    block_of_key = np.arange(s)[:, None] // SEL_BLOCK == np.arange(LANES)[None, :]
    onehot = jnp.asarray(block_of_key.astype(np.float32), dtype=BF16)
    kern = functools.partial(_nsa_kernel, tq=tq, top=min(SEL_TOPK, ns))
    return pl.pallas_call(
        kern,
        grid=(b, s // tq),
        in_specs=[pl.BlockSpec((1, tq, NSA_D), lambda i, j: (i, j, 0)),
                  pl.BlockSpec((1, LANES, tq), lambda i, j: (i, 0, j)),
                  pl.BlockSpec((1, nc, LANES), lambda i, j: (i, 0, 0)),
                  pl.BlockSpec((1, LANES, nc), lambda i, j: (i, 0, 0)),
                  pl.BlockSpec((1, s, 2 * LANES), lambda i, j: (i, 0, 0)),
                  pl.BlockSpec((1, 2, s // KEY_TILE, LANES, KEY_TILE), lambda i, j: (i, 0, 0, 0, 0)),
                  pl.BlockSpec(overlap_t.shape, lambda i, j: (0, 0)),
                  pl.BlockSpec(onehot.shape, lambda i, j: (0, 0))],
        out_specs=pl.BlockSpec((1, tq, NSA_D), lambda i, j: (i, j, 0)),
        out_shape=jax.ShapeDtypeStruct((b, s, NSA_D), BF16),
        scratch_shapes=[pltpu.VMEM((NSA_GROUPS, s // SWEEP, SWEEP, NSA_HPG * tq), F32)],
        compiler_params=_cparams(("arbitrary", "arbitrary")),
        name="nsa_attention",
    )(nq, gate_t, kc, vct, nk, vt, overlap_t, onehot)


def _softplus(z):
    e = jnp.exp(-jnp.abs(z))
    return jnp.maximum(z, 0.0) + jnp.where(e < 2.0 ** -12, e, jnp.log(1.0 + e))


def _sb_kernel(q_ref, k_ref, vt_ref, o_ref, *, tq, n_heads):
    tk = KEY_TILE
    pair = pl.program_id(1)
    qi = pl.program_id(2)
    q = q_ref[0]
    lane = _iota((1, LANES), 1)
    t_row1 = qi * tq + _iota((1, tq), 1)
    row = _iota((LANES, 1), 0)
    later = jnp.where(_iota((tk, tk), 1) > _iota((tk, tk), 0), 1.0, 0.0).astype(BF16)

    def run(nh):
        qs = jnp.concatenate([jnp.where((lane // HEAD_DIM) == a, q, jnp.zeros((), BF16))
                              for a in range(nh)], axis=0)
        t_row = _lanes([t_row1] * nh)

        def tile(kt, run_row, acc, diagonal):
            k0 = pl.multiple_of(kt * tk, tk)
            z = _dot_nt(k_ref[0, pl.ds(k0, tk), :], qs)
            lneg = -_softplus(z)
            if diagonal:
                msk = (k0 + _iota((tk, 1), 0)) < t_row
                lm = jnp.where(msk, lneg, 0.0)
            else:
                lm = lneg
            hi, lo = _hi_lo(lm)
            between = _dot(later, hi) + _dot(later, lo) + run_row
            a_w = jnp.exp(lneg + z + between)
            if diagonal:
                a_w = jnp.where(msk, a_w, 0.0)
            acc = acc + _dot(vt_ref[0, 0, kt], a_w.astype(BF16))
            return run_row + jnp.sum(lm, axis=0, keepdims=True), acc

        run_row, acc = tile(qi, jnp.zeros((1, nh * tq), F32), jnp.zeros((LANES, nh * tq), F32), True)

        def alive(c):
            return (c[0] >= 0) & (jnp.max(c[1]) > SB_DEAD)

        def step(c):
            r, a = tile(c[0], c[1], c[2], False)
            return c[0] - 1, r, a

        acc = lax.while_loop(alive, step, (qi - 1, run_row, acc))[2]
        if nh == 2:
            o_t = jnp.where(row < HEAD_DIM, acc[:, 0:tq], acc[:, tq:2 * tq])
        else:
            o_t = jnp.where(row < HEAD_DIM, acc, 0.0)
        o_ref[0] = o_t.T.astype(BF16)

    @pl.when(pair * 2 + 1 < n_heads)
    def _():
        run(2)

    @pl.when(pair * 2 + 1 >= n_heads)
    def _():
        run(1)


def _sb_attention(sb, vt):
    b, s, _ = sb.shape
    tq = KEY_TILE
    kern = functools.partial(_sb_kernel, tq=tq, n_heads=SB_HEADS)
    return pl.pallas_call(
        kern,
        grid=(b, PAIR_SLABS, s // tq),
        in_specs=[pl.BlockSpec((1, tq, LANES), lambda i, p, j: (i, j, p)),
                  pl.BlockSpec((1, s, LANES), lambda i, p, j: (i, 0, PAIR_SLABS + p)),
                  pl.BlockSpec((1, 1, s // KEY_TILE, LANES, KEY_TILE), lambda i, p, j: (i, 2 + p, 0, 0, 0))],
        out_specs=pl.BlockSpec((1, tq, LANES), lambda i, p, j: (i, j, p)),
        out_shape=jax.ShapeDtypeStruct((b, s, PAIR_D), BF16),
        compiler_params=_cparams(("arbitrary", "arbitrary", "arbitrary")),
        name="sb_attention",
    )(sb, sb, vt)


def _mla_kernel(qn_ref, qr_ref, kn_ref, kpe_ref, vt_ref, o_ref, s_buf, *, tq, n_heads):
    pair = pl.program_id(1)
    t0 = pl.program_id(2) * tq
    per_sweep = SWEEP // KEY_TILE
    q = jnp.concatenate([qn_ref[0], qr_ref[0]], axis=1)
    lane2 = _iota((1, 2 * LANES), 1)
    lane_head = jnp.where(lane2 < LANES, lane2 // MLA_NOPE, (lane2 - LANES) // MLA_ROPE)
    t_row1 = t0 + _iota((1, tq), 1)
    row = _iota((LANES, 1), 0)
    scale = (MLA_NOPE + MLA_ROPE) ** -0.5
    ones_rows = jnp.ones((ONES_ROWS, KEY_TILE), BF16)
    n_full = t0 // SWEEP
    n_it = (t0 + tq + SWEEP - 1) // SWEEP

    def run(nh):
        m = nh * tq
        qs = jnp.concatenate([jnp.where(lane_head == a, q, jnp.zeros((), BF16))
                              for a in range(nh)], axis=0)
        t_row = _lanes([t_row1] * nh)

        def scores(it, m8, diagonal):
            k0 = pl.multiple_of(it * SWEEP, SWEEP)
            k = jnp.concatenate([kn_ref[0, pl.ds(k0, SWEEP), :], kpe_ref[0, pl.ds(k0, SWEEP), :]], axis=1)
            sc = _dot_nt(k, qs) * (scale * LOG2E)
            if diagonal:
                sc = jnp.where((k0 + _iota((SWEEP, 1), 0)) <= t_row, sc, -jnp.inf)
            s_buf[it, :, 0:m] = sc
            return jnp.maximum(m8, jnp.max(sc.reshape(-1, 8, m), axis=0))

        m8 = _sweep_loop(0, n_full, lambda it, c: scores(it, c, False), jnp.full((8, m), M_FLOOR, F32))
        m8 = lax.fori_loop(n_full, n_it, lambda it, c: scores(it, c, True), m8)
        m_f = jnp.max(m8, axis=0, keepdims=True)

        def values(it, acc):
            pb = jnp.exp2(s_buf[it, :, 0:m] - m_f).astype(BF16)
            for h in range(per_sweep):
                v_aug = jnp.concatenate([vt_ref[0, 0, it * per_sweep + h], ones_rows], axis=0)
                acc = acc + _dot(v_aug, pb[h * KEY_TILE:(h + 1) * KEY_TILE])
            return acc

        acc = _sweep_loop(0, n_it, values, jnp.zeros((LANES + ONES_ROWS, m), F32))
        o = acc[0:LANES] / acc[LANES:LANES + 1]
        if nh == 2:
            o_t = jnp.where(row < MLA_V, o[:, 0:tq], o[:, tq:2 * tq])
        else:
            o_t = jnp.where(row < MLA_V, o, 0.0)
        o_ref[0] = o_t.T.astype(BF16)

    @pl.when(pair * 2 + 1 < n_heads)
    def _():
        run(2)

    @pl.when(pair * 2 + 1 >= n_heads)
    def _():
        run(1)


def _mla_attention(mla, vt, tq=SWEEP):
    b, s, _ = mla.shape
    kern = functools.partial(_mla_kernel, tq=tq, n_heads=MLA_HEADS)
    ps = PAIR_SLABS
    return pl.pallas_call(
        kern,
        grid=(b, ps, s // tq),
        in_specs=[pl.BlockSpec((1, tq, LANES), lambda i, p, j: (i, j, p)),
                  pl.BlockSpec((1, tq, LANES), lambda i, p, j: (i, j, ps + p)),
                  pl.BlockSpec((1, s, LANES), lambda i, p, j: (i, 0, 2 * ps + p)),
                  pl.BlockSpec((1, s, LANES), lambda i, p, j: (i, 0, 3 * ps)),
                  pl.BlockSpec((1, 1, s // KEY_TILE, LANES, KEY_TILE),
                               lambda i, p, j: (i, 2 + ps + p, 0, 0, 0))],
        out_specs=pl.BlockSpec((1, tq, LANES), lambda i, p, j: (i, j, p)),
        out_shape=jax.ShapeDtypeStruct((b, s, PAIR_D), BF16),
        scratch_shapes=[pltpu.VMEM((s // SWEEP, SWEEP, 2 * tq), F32)],
        compiler_params=_cparams(("arbitrary", "arbitrary", "arbitrary")),
        name="mla_attention",
    )(mla, mla, mla, mla, vt)


def _outproj_kernel(oa_ref, ob_ref, oc_ref, z_ref, x_ref, mod_ref, w_ref, fg_ref, o_ref, *, final):
    mix = jnp.concatenate([oa_ref[0], ob_ref[0], oc_ref[0]], axis=1).astype(F32) * jax.nn.silu(z_ref[0])
    y = _dot(mix.astype(BF16), w_ref[...])
    xn = x_ref[0] + mod_ref[0, 2:3, :] * y
    if final:
        ms = jnp.mean(xn * xn, axis=-1, keepdims=True)
        xn = xn * lax.rsqrt(ms + EPS) * fg_ref[...]
    o_ref[0] = xn


def _out_projection(oa, ob, oc, z, x, mod3, w_out_p, final_g, final, ts=256):
    b, s, d = x.shape
    row = lambda n: pl.BlockSpec((1, ts, n), lambda i, j: (i, j, 0))
    full = lambda a: pl.BlockSpec(a.shape, lambda i, j: (0,) * a.ndim)
    return pl.pallas_call(
        functools.partial(_outproj_kernel, final=final),
        grid=(b, s // ts),
        in_specs=[row(NSA_D), row(PAIR_D), row(PAIR_D), row(3 * PAIR_D), row(d),
                  pl.BlockSpec((1, 3, d), lambda i, j: (i, 0, 0)), full(w_out_p), full(final_g)],
        out_specs=row(d),
        out_shape=jax.ShapeDtypeStruct((b, s, d), F32),
        compiler_params=_cparams(("arbitrary", "arbitrary")),
        name="out_projection",
    )(oa, ob, oc, z, x, mod3, w_out_p, final_g)


def _overlap_matrix_t(s):
    n_cmp = (s - CMP_LEN) // CMP_STRIDE + 1
    n_sel = s // SEL_BLOCK
    c_start = np.arange(s // CMP_STRIDE) * CMP_STRIDE
    s_start = np.arange(n_sel) * SEL_BLOCK
    ov = ((c_start[None, :] < s_start[:, None] + SEL_BLOCK)
          & (c_start[None, :] + CMP_LEN > s_start[:, None])
          & (np.arange(s // CMP_STRIDE)[None, :] < n_cmp))
    return jnp.asarray(ov.astype(np.float32), dtype=BF16)


def _mixer_layer(x, mod3, tabs, overlap_t, norm_g, w_in, pos_k, pos_v, ck_w1, ck_w2, cv_w1, cv_w2,
                 q_norm, w_uq, kv_norm, w_ukv, w_out, final_g, final):
    b, s, d = x.shape
    w_in_p = _take_cols(w_in, _in_cols()).astype(BF16)
    w_uq_p = jnp.pad(_take_cols(w_uq, _uq_cols()), ((0, Q_RANK_PAD - MLA_Q_RANK), (0, 0))).astype(BF16)
    w_ukv_p = _take_cols(w_ukv, _ukv_cols()).astype(BF16)
    q_norm_p = jnp.pad(q_norm, (0, Q_RANK_PAD - MLA_Q_RANK)).reshape(1, Q_RANK_PAD)
    w_out_p = _take_cols(w_out.T, _out_rows()).T.astype(BF16)

    nq, nk, cmp_tok, gate_t, z, sb, mla, vt = _in_projection(
        x, mod3, norm_g.reshape(1, d), w_in_p, q_norm_p, w_uq_p, kv_norm.reshape(1, -1), w_ukv_p, tabs)

    chunk_w = CMP_STRIDE * LANES
    kc_chunks = cmp_tok[:, :, 0:LANES].reshape(b, s // CMP_STRIDE, chunk_w)
    vc_chunks = cmp_tok[:, :, LANES:].reshape(b, s // CMP_STRIDE, chunk_w)
    pos4 = jnp.concatenate([_cmp_pos(pos_k), _cmp_pos(pos_v)], axis=0)
    kc, vct = _nsa_compress(kc_chunks, vc_chunks, pos4,
                            _cmp_weights(ck_w1, ck_w2), _cmp_weights(cv_w1, cv_w2))

    o_a = _nsa_attention(nq, gate_t, kc, vct, nk, vt, overlap_t)
    o_b = _sb_attention(sb, vt)
    o_c = _mla_attention(mla, vt)
    return _out_projection(o_a, o_b, o_c, z, x, mod3, w_out_p, final_g.reshape(1, d), final)


def kernel(x, c, positions, ada_w, ada_b, norm_g, w_in, nsa_pos_k, nsa_pos_v, nsa_ck_w1, nsa_ck_w2,
           nsa_cv_w1, nsa_cv_w2, mla_q_norm, mla_w_uq, mla_kv_norm, mla_w_ukv, w_out, final_norm):
    b, s, d = x.shape
    depth = w_in.shape[0]
    assert s % SWEEP == 0 and s >= WINDOW + 2 * KEY_TILE
    mod = _adaln_mod(c, ada_w, ada_b).reshape(depth, b, 3, d)
    tabs = _rope_tables(positions)
    overlap_t = _overlap_matrix_t(s)
    for l in range(depth):
        x = _mixer_layer(x, mod[l], tabs, overlap_t, norm_g[l], w_in[l], nsa_pos_k[l], nsa_pos_v[l],
                         nsa_ck_w1[l], nsa_ck_w2[l], nsa_cv_w1[l], nsa_cv_w2[l],
                         mla_q_norm[l], mla_w_uq[l], mla_kv_norm[l], mla_w_ukv[l], w_out[l],
                         final_norm, final=(l == depth - 1))
    return x
```

```python
import functools

import numpy as np
import jax
import jax.numpy as jnp
from jax import lax
from jax.experimental import pallas as pl
from jax.experimental.pallas import tpu as pltpu

F32 = jnp.float32
BF16 = jnp.bfloat16

LANES = 128
HEAD_DIM = 64
NSA_HEADS = 6
NSA_GROUPS = 2
NSA_HPG = NSA_HEADS // NSA_GROUPS
SB_HEADS = 5
MLA_HEADS = 5
MLA_NOPE = 64
MLA_ROPE = 32
MLA_V = 64
MLA_Q_RANK = 192
MLA_KV_RANK = 128
CMP_LEN = 32
CMP_STRIDE = 16
CMP_HIDDEN = 128
SEL_BLOCK = 64
SEL_TOPK = 16
WINDOW = 512
ROPE_THETA = 10000.0
EPS = 1e-6
SEL_FORCE = 1e9
M_FLOOR = -1e30
SB_DEAD = -110.0
LOG2E = 1.4426950408889634
NEG_MASK = -1e30
ONES_ROWS = 16

NSA_D = NSA_HEADS * HEAD_DIM
PAIR_SLABS = 3
PAIR_D = PAIR_SLABS * LANES
Q_RANK_PAD = 256

KEY_TILE = 256
SWEEP = 2 * KEY_TILE
VT_SLABS = 8

C_NQ = 0
C_NKV = C_NQ + NSA_D
C_CMP = C_NKV + 4 * LANES
C_GATE = C_CMP + 2 * LANES
C_Z = C_GATE + LANES
C_SB = C_Z + 3 * PAIR_D
C_CQ = C_SB + 3 * PAIR_D
C_CKV = C_CQ + Q_RANK_PAD
C_KR = C_CKV + LANES
D_IN_PAD = C_KR + LANES

VMEM_LIMIT = 52 * 1024 * 1024


def _cparams(sem):
    return pltpu.CompilerParams(dimension_semantics=sem, vmem_limit_bytes=VMEM_LIMIT)


def _dot(a, b):
    return jnp.dot(a, b, preferred_element_type=F32)


def _dot_nt(a, b):
    return lax.dot_general(a, b, (((1,), (1,)), ((), ())), preferred_element_type=F32)


def _iota(shape, dim):
    return lax.broadcasted_iota(jnp.int32, shape, dim)


def _rope(x, cos, sin_signed, width):
    half = width // 2
    lane = _iota(x.shape, 1)
    first = (lane % width) < half
    rot = jnp.where(first, pltpu.roll(x, LANES - half, 1), pltpu.roll(x, half, 1))
    return x * cos + rot * sin_signed


def _hi_lo(x):
    hi = x.astype(BF16)
    return hi, (x - hi.astype(F32)).astype(BF16)


def _lanes(parts):
    return parts[0] if len(parts) == 1 else jnp.concatenate(parts, axis=1)


def _sweep_loop(lo, hi, step, carry):
    def block(width):
        def body(i, c):
            for u in range(width):
                c = step(i + u, c)
            return c
        return body

    for width in (4, 2):
        n = (hi - lo) // width
        carry = lax.fori_loop(0, n, lambda i, c, w=width, base=lo: block(w)(base + w * i, c), carry)
        lo = lo + width * n
    return lax.fori_loop(lo, hi, step, carry)


def _in_cols():
    widths = (NSA_D, 128, 128, 128, 128, 128, 128, NSA_HEADS * 3, NSA_D,
              320, 320, 320, 320, MLA_Q_RANK, MLA_KV_RANK, MLA_ROPE, 320)
    off = np.concatenate([[0], np.cumsum(widths)])
    (o_nq, o_kc, o_vc, o_ks, o_vs, o_kw, o_vw, o_gate, o_nz,
     o_sq, o_sk, o_sv, o_sz, o_cq, o_ckv, o_kr, o_mz) = [int(v) for v in off[:-1]]
    cols = -np.ones((D_IN_PAD,), np.int64)

    def put(dst, src, n):
        cols[dst:dst + n] = src + np.arange(n)

    def put_nsa(dst, src):
        for i in range(NSA_HPG):
            for g in range(NSA_GROUPS):
                put(dst + i * LANES + g * HEAD_DIM, src + (g * NSA_HPG + i) * HEAD_DIM, HEAD_DIM)

    put_nsa(C_NQ, o_nq)
    put(C_NKV, o_ks, 128)
    put(C_NKV + 128, o_vs, 128)
    put(C_NKV + 256, o_kw, 128)
    put(C_NKV + 384, o_vw, 128)
    put(C_CMP, o_kc, 128)
    put(C_CMP + 128, o_vc, 128)
    put(C_GATE, o_gate, NSA_HEADS * 3)
    put_nsa(C_Z, o_nz)
    put(C_Z + PAIR_D, o_sz, 320)
    put(C_Z + 2 * PAIR_D, o_mz, 320)
    put(C_SB, o_sq, 320)
    put(C_SB + PAIR_D, o_sk, 320)
    put(C_SB + 2 * PAIR_D, o_sv, 320)
    put(C_CQ, o_cq, MLA_Q_RANK)
    put(C_CKV, o_ckv, MLA_KV_RANK)
    put(C_KR, o_kr, MLA_ROPE)
    put(C_KR + MLA_ROPE, o_kr, MLA_ROPE)
    return cols


def _take_cols(w, cols, axis=1):
    cols = np.asarray(cols)
    parts, i = [], 0
    while i < len(cols):
        j = i + 1
        if cols[i] < 0:
            while j < len(cols) and cols[j] < 0:
                j += 1
            shape = list(w.shape)
            shape[axis] = j - i
            parts.append(jnp.zeros(shape, w.dtype))
        else:
            while j < len(cols) and cols[j] == cols[j - 1] + 1:
                j += 1
            parts.append(lax.slice_in_dim(w, int(cols[i]), int(cols[i]) + j - i, axis=axis))
        i = j
    return jnp.concatenate(parts, axis=axis)


def _uq_cols():
    cols = -np.ones((2 * PAIR_D,), np.int64)
    for h in range(MLA_HEADS):
        p, a = divmod(h, 2)
        base = h * (MLA_NOPE + MLA_ROPE)
        cols[p * LANES + a * MLA_NOPE: p * LANES + (a + 1) * MLA_NOPE] = base + np.arange(MLA_NOPE)
        d = PAIR_D + p * LANES + a * MLA_ROPE
        cols[d: d + MLA_ROPE] = base + MLA_NOPE + np.arange(MLA_ROPE)
    return cols


def _ukv_cols():
    cols = -np.ones((2 * PAIR_D,), np.int64)
    for h in range(MLA_HEADS):
        base = h * (MLA_NOPE + MLA_V)
        cols[h * MLA_NOPE: (h + 1) * MLA_NOPE] = base + np.arange(MLA_NOPE)
        cols[PAIR_D + h * MLA_V: PAIR_D + (h + 1) * MLA_V] = base + MLA_NOPE + np.arange(MLA_V)
    return cols


def _out_rows():
    rows = -np.ones((3 * PAIR_D,), np.int64)
    for i in range(NSA_HPG):
        for g in range(NSA_GROUPS):
            d = i * LANES + g * HEAD_DIM
            rows[d:d + HEAD_DIM] = (g * NSA_HPG + i) * HEAD_DIM + np.arange(HEAD_DIM)
    rows[PAIR_D:PAIR_D + 320] = NSA_D + np.arange(320)
    rows[2 * PAIR_D:2 * PAIR_D + 320] = NSA_D + 320 + np.arange(320)
    return rows


def _cmp_weights(w1, w2):
    half = CMP_LEN // 2
    w1r = w1.reshape(CMP_LEN, HEAD_DIM, CMP_HIDDEN)
    z = jnp.zeros((half, HEAD_DIM, CMP_HIDDEN), w1.dtype)

    def build(part):
        g0 = jnp.concatenate([part, z], axis=1)
        g1 = jnp.concatenate([z, part], axis=1)
        return jnp.concatenate([g0.reshape(half * LANES, CMP_HIDDEN),
                                g1.reshape(half * LANES, CMP_HIDDEN)], axis=1)

    zz = jnp.zeros_like(w2)
    w2bd = jnp.concatenate([jnp.concatenate([w2, zz], axis=1),
                            jnp.concatenate([zz, w2], axis=1)], axis=0)
    return build(w1r[:half]).astype(BF16), build(w1r[half:]).astype(BF16), w2bd.astype(BF16)


def _cmp_pos(pos):
    half = CMP_LEN // 2
    tiled = jnp.concatenate([pos, pos], axis=1)
    return jnp.stack([tiled[:half].reshape(half * LANES), tiled[half:].reshape(half * LANES)])


def _mod_kernel(c_ref, w_ref, b_ref, o_ref):
    a = jax.nn.silu(c_ref[...])
    o_ref[0] = _dot(a, w_ref[0]) + b_ref[0]


def _adaln_mod(c, ada_w, ada_b):
    depth, d, n = ada_w.shape
    b = c.shape[0]
    tn = 1024
    return pl.pallas_call(
        _mod_kernel,
        grid=(depth, n // tn),
        in_specs=[pl.BlockSpec((b, d), lambda l, j: (0, 0)),
                  pl.BlockSpec((1, d, tn), lambda l, j: (l, 0, j)),
                  pl.BlockSpec((1, 1, tn), lambda l, j: (l, 0, j))],
        out_specs=pl.BlockSpec((1, b, tn), lambda l, j: (l, 0, j)),
        out_shape=jax.ShapeDtypeStruct((depth, b, n), F32),
        compiler_params=_cparams(("arbitrary", "arbitrary")),
        name="adaln_mod",
    )(c, ada_w, ada_b.reshape(depth, 1, n))


def _rope_kernel(pos_ref, inv_ref, c64_ref, s64_ref, c32_ref, s32_ref):
    pos = pos_ref[0].astype(F32)
    lane = _iota((1, LANES), 1)
    ang = pos * inv_ref[0:1, :]
    n64, n32 = HEAD_DIM // 2, MLA_ROPE // 2

    def tile_lanes(x, first, period):
        y = jnp.where((lane >= first) & (lane < first + period), x, 0.0)
        if first:
            y = pltpu.roll(y, LANES - first, 1)
        width = period
        while width < LANES:
            y = y + pltpu.roll(y, width, 1)
            width *= 2
        return y

    cos, sin = jnp.cos(ang), jnp.sin(ang)
    c64_ref[0] = tile_lanes(cos, 0, n64)
    s64 = tile_lanes(sin, 0, n64)
    s64_ref[0] = jnp.where((lane % HEAD_DIM) < n64, -s64, s64)
    c32_ref[0] = tile_lanes(cos, n64, n32)
    s32 = tile_lanes(sin, n64, n32)
    s32_ref[0] = jnp.where((lane % MLA_ROPE) < n32, -s32, s32)


def _rope_tables(positions):
    b, s = positions.shape
    ts = 512

    def inv(half):
        return ROPE_THETA ** (-jnp.arange(half, dtype=F32) / half)

    packed = jnp.concatenate([inv(HEAD_DIM // 2), inv(MLA_ROPE // 2)])
    inv_tab = jnp.zeros((8, LANES), F32).at[0, 0:packed.shape[0]].set(packed)
    tab = jax.ShapeDtypeStruct((b, s, LANES), F32)
    spec = pl.BlockSpec((1, ts, LANES), lambda i, j: (i, j, 0))
    return pl.pallas_call(
        _rope_kernel,
        grid=(b, s // ts),
        in_specs=[pl.BlockSpec((1, ts, 1), lambda i, j: (i, j, 0)),
                  pl.BlockSpec((8, LANES), lambda i, j: (0, 0))],
        out_specs=[spec] * 4,
        out_shape=[tab] * 4,
        compiler_params=_cparams(("arbitrary", "arbitrary")),
        name="rope_tables",
    )(positions.reshape(b, s, 1), inv_tab)


def _inproj_kernel(x_ref, mod_ref, g_ref, w_ref, qn_ref, wuq_ref, kvn_ref, wukv_ref,
                   c64_ref, s64_ref, c32_ref, s32_ref,
                   nq_ref, nk_ref, cmp_ref, gatet_ref, z_ref, sb_ref, mla_ref, vt_ref):
    x = x_ref[0]
    ms = jnp.mean(x * x, axis=-1, keepdims=True)
    y = x * lax.rsqrt(ms + EPS) * g_ref[...]
    h = y * (1.0 + mod_ref[0, 1:2, :]) + mod_ref[0, 0:1, :]
    hb = h.astype(BF16)

    def proj(c0, n):
        return _dot(hb, w_ref[:, c0:c0 + n])

    def slab(val, c0, j):
        return val[:, c0 + j * LANES:c0 + (j + 1) * LANES]

    def put_vt(slab_id, val):
        for sub in range(val.shape[0] // KEY_TILE):
            vt_ref[0, slab_id, sub] = val[sub * KEY_TILE:(sub + 1) * KEY_TILE].T.astype(BF16)

    c64, s64 = c64_ref[0], s64_ref[0]
    c32, s32 = c32_ref[0], s32_ref[0]
    q_scale = HEAD_DIM ** -0.5

    nsa = proj(C_NQ, C_Z - C_NQ)
    for i in range(NSA_HPG):
        q = _rope(slab(nsa, C_NQ, i), c64, s64, HEAD_DIM)
        nq_ref[0, :, i * LANES:(i + 1) * LANES] = (q * q_scale).astype(BF16)
    nk_ref[0, :, 0:LANES] = _rope(slab(nsa, C_NKV, 0), c64, s64, HEAD_DIM).astype(BF16)
    put_vt(0, slab(nsa, C_NKV, 1))
    nk_ref[0, :, LANES:2 * LANES] = _rope(slab(nsa, C_NKV, 2), c64, s64, HEAD_DIM).astype(BF16)
    put_vt(1, slab(nsa, C_NKV, 3))
    cmp_ref[0, :, 0:LANES] = _rope(slab(nsa, C_CMP, 0), c64, s64, HEAD_DIM)
    cmp_ref[0, :, LANES:2 * LANES] = slab(nsa, C_CMP, 1)
    gatet_ref[0] = jax.nn.sigmoid(slab(nsa, C_GATE, 0)).T
    z_ref[0] = proj(C_Z, 3 * PAIR_D).astype(BF16)
    sbp = proj(C_SB, 3 * PAIR_D)
    sb_ref[0, :, 0:PAIR_D] = (sbp[:, 0:PAIR_D] * q_scale).astype(BF16)
    sb_ref[0, :, PAIR_D:2 * PAIR_D] = sbp[:, PAIR_D:2 * PAIR_D].astype(BF16)
    for p in range(PAIR_SLABS):
        put_vt(2 + p, slab(sbp, 2 * PAIR_D, p))

    lat = proj(C_CQ, D_IN_PAD - C_CQ)
    cq = lat[:, 0:Q_RANK_PAD]
    cqn = cq * lax.rsqrt(jnp.sum(cq * cq, axis=-1, keepdims=True) / MLA_Q_RANK + EPS) * qn_ref[...]
    q2 = _dot(cqn.astype(BF16), wuq_ref[...])
    mla_ref[0, :, 0:PAIR_D] = q2[:, 0:PAIR_D].astype(BF16)
    for p in range(PAIR_SLABS):
        r = _rope(q2[:, PAIR_D + p * LANES:PAIR_D + (p + 1) * LANES], c32, s32, MLA_ROPE)
        mla_ref[0, :, PAIR_D + p * LANES:PAIR_D + (p + 1) * LANES] = r.astype(BF16)
    ckv = lat[:, Q_RANK_PAD:Q_RANK_PAD + LANES]
    ckvn = ckv * lax.rsqrt(jnp.mean(ckv * ckv, axis=-1, keepdims=True) + EPS) * kvn_ref[...]
    kv2 = _dot(ckvn.astype(BF16), wukv_ref[...])
    mla_ref[0, :, 2 * PAIR_D:3 * PAIR_D] = kv2[:, 0:PAIR_D].astype(BF16)
    for p in range(PAIR_SLABS):
        put_vt(2 + PAIR_SLABS + p, slab(kv2, PAIR_D, p))
    kr = _rope(lat[:, Q_RANK_PAD + LANES:Q_RANK_PAD + 2 * LANES], c32, s32, MLA_ROPE)
    mla_ref[0, :, 3 * PAIR_D:3 * PAIR_D + LANES] = kr.astype(BF16)


def _in_projection(x, mod3, norm_g, w_in_p, q_norm_p, w_uq_p, kv_norm, w_ukv_p, tabs):
    b, s, d = x.shape
    ts = 2 * KEY_TILE
    row = lambda n: pl.BlockSpec((1, ts, n), lambda i, j: (i, j, 0))
    full = lambda a: pl.BlockSpec(a.shape, lambda i, j: (0,) * a.ndim, pipeline_mode=pl.Buffered(1))
    outs = [(NSA_D, BF16), (2 * LANES, BF16), (2 * LANES, F32), None,
            (3 * PAIR_D, BF16), (2 * PAIR_D, BF16), (3 * PAIR_D + LANES, BF16), None]
    out_specs = [row(o[0]) if o else None for o in outs]
    out_shape = [jax.ShapeDtypeStruct((b, s, o[0]), o[1]) if o else None for o in outs]
    out_specs[3] = pl.BlockSpec((1, LANES, ts), lambda i, j: (i, 0, j))
    out_shape[3] = jax.ShapeDtypeStruct((b, LANES, s), F32)
    out_specs[7] = pl.BlockSpec((1, VT_SLABS, ts // KEY_TILE, LANES, KEY_TILE), lambda i, j: (i, 0, j, 0, 0))
    out_shape[7] = jax.ShapeDtypeStruct((b, VT_SLABS, s // KEY_TILE, LANES, KEY_TILE), BF16)
    return pl.pallas_call(
        _inproj_kernel,
        grid=(b, s // ts),
        in_specs=[row(d), pl.BlockSpec((1, 3, d), lambda i, j: (i, 0, 0)), full(norm_g),
                  full(w_in_p), full(q_norm_p), full(w_uq_p), full(kv_norm), full(w_ukv_p),
                  row(LANES), row(LANES), row(LANES), row(LANES)],
        out_specs=out_specs,
        out_shape=out_shape,
        compiler_params=_cparams(("arbitrary", "arbitrary")),
        name="in_projection",
    )(x, mod3, norm_g, w_in_p, q_norm_p, w_uq_p, kv_norm, w_ukv_p, *tabs)


def _cmp_kernel(kc_ref, vc_ref, pos_ref, wkt_ref, wkb_ref, wk2_ref, wvt_ref, wvb_ref, wv2_ref,
                kc_out, vct_out):
    def compress(chunks, pos_row, wt_ref, wb_ref, w2_ref):
        a = _dot((chunks + pos_ref[pos_row:pos_row + 1, :]).astype(BF16), wt_ref[...])
        bm = _dot((chunks + pos_ref[pos_row + 1:pos_row + 2, :]).astype(BF16), wb_ref[...])
        n = chunks.shape[0]
        pre = a + pltpu.roll(bm, n - 1, 0)
        return _dot(jax.nn.silu(pre).astype(BF16), w2_ref[...])

    kc_out[0] = compress(kc_ref[0], 0, wkt_ref, wkb_ref, wk2_ref).astype(BF16)
    vct_out[0] = compress(vc_ref[0], 2, wvt_ref, wvb_ref, wv2_ref).T.astype(BF16)


def _nsa_compress(kc_chunks, vc_chunks, pos4, wk, wv):
    b, nc, cw = kc_chunks.shape
    full = lambda a: pl.BlockSpec(a.shape, lambda i: (0,) * a.ndim)
    chunk = pl.BlockSpec((1, nc, cw), lambda i: (i, 0, 0))
    return pl.pallas_call(
        _cmp_kernel,
        grid=(b,),
        in_specs=[chunk, chunk, full(pos4)] + [full(w) for w in wk] + [full(w) for w in wv],
        out_specs=[pl.BlockSpec((1, nc, LANES), lambda i: (i, 0, 0)),
                   pl.BlockSpec((1, LANES, nc), lambda i: (i, 0, 0))],
        out_shape=[jax.ShapeDtypeStruct((b, nc, LANES), BF16),
                   jax.ShapeDtypeStruct((b, LANES, nc), BF16)],
        compiler_params=_cparams(("arbitrary",)),
        name="nsa_compress",
    )(kc_chunks, vc_chunks, pos4, *wk, *wv)


def _nsa_kernel(q_ref, gt_ref, kc_ref, vct_ref, k_ref, vt_ref, ovt_ref, oh_ref, o_ref, s_buf,
                *, tq, top):
    t0 = pl.program_id(1) * tq
    nc = kc_ref.shape[1]
    ns = ovt_ref.shape[0]
    per_sweep = SWEEP // KEY_TILE
    win_tiles = -(-(WINDOW + tq + max(KEY_TILE - tq, 0)) // KEY_TILE)
    ones_rows = jnp.ones((ONES_ROWS, KEY_TILE), BF16)
    m = NSA_HPG * tq
    q = q_ref[0]
    gt = gt_ref[0]
    lane = _iota((1, LANES), 1)
    t_row = t0 + _iota((1, tq), 1)

    def heads(a):
        return _lanes([a] * NSA_HPG)

    t_row3 = heads(t_row)
    vis_bias = jnp.where((_iota((nc, 1), 0) * CMP_STRIDE + (CMP_LEN - 1)) <= t_row, 0.0, -jnp.inf)
    j_col = _iota((ns, 1), 0)
    cur = t_row // SEL_BLOCK
    forced = (j_col == 0) | (j_col == cur) | (j_col == cur - 1)
    valid = j_col * SEL_BLOCK <= t_row
    n_full = t0 // SWEEP
    n_it = (t0 + tq + SWEEP - 1) // SWEEP
    wt = jnp.maximum(t0 - WINDOW, 0) // KEY_TILE
    w0 = pl.multiple_of(wt * KEY_TILE, KEY_TILE)
    wlen = win_tiles * KEY_TILE
    tokw = w0 + _iota((wlen, 1), 0)
    win_bias = jnp.where((tokw <= t_row) & (tokw > t_row - WINDOW), 0.0, -jnp.inf)

    def gate_row(g, br):
        return _lanes([gt[(g * NSA_HPG + i) * 3 + br:(g * NSA_HPG + i) * 3 + br + 1, :]
                       for i in range(NSA_HPG)])

    def value_rows(g, slab_id, kt):
        return jnp.concatenate([vt_ref[0, slab_id, kt, g * HEAD_DIM:(g + 1) * HEAD_DIM, :], ones_rows],
                               axis=0)

    def prologue(g):
        qs = jnp.concatenate([jnp.where((lane // HEAD_DIM) == g, q[:, i * LANES:(i + 1) * LANES],
                                        jnp.zeros((), BF16)) for i in range(NSA_HPG)], axis=0)

        sw = _dot_nt(k_ref[0, pl.ds(w0, wlen), LANES:2 * LANES], qs) + heads(win_bias)
        pwb = jnp.exp(sw - jnp.max(sw, axis=0, keepdims=True)).astype(BF16)
        acc_w = jnp.zeros((HEAD_DIM + ONES_ROWS, m), F32)
        for h in range(win_tiles):
            acc_w = acc_w + _dot(value_rows(g, 1, wt + h), pwb[h * KEY_TILE:(h + 1) * KEY_TILE])
        o_w = acc_w[0:HEAD_DIM] / acc_w[HEAD_DIM:HEAD_DIM + 1]

        s = _dot_nt(kc_ref[0], qs) + heads(vis_bias)
        mx = jnp.maximum(jnp.max(s, axis=0, keepdims=True), M_FLOOR)
        p = jnp.exp(s - mx)
        l = jnp.sum(p, axis=0, keepdims=True)
        pc = p * jnp.where(l > 0.0, 1.0 / l, 0.0)
        o_c = _dot(vct_ref[0, g * HEAD_DIM:(g + 1) * HEAD_DIM, :], pc.astype(BF16))

        hi, lo = _hi_lo(pc[:, 0:tq] + pc[:, tq:2 * tq] + pc[:, 2 * tq:3 * tq])
        imp_sel = _dot(ovt_ref[...], hi) + _dot(ovt_ref[...], lo)
        rank = _rank(jnp.where(valid, jnp.where(forced, SEL_FORCE, imp_sel), -SEL_FORCE))

        member = jnp.concatenate([jnp.where(rank < top, 0.0, NEG_MASK), jnp.zeros((LANES - ns, tq), F32)],
                                 axis=0).T.astype(BF16)
        qs_aug = jnp.concatenate([qs, jnp.concatenate([member] * NSA_HPG, axis=0)], axis=1)
        return qs_aug, gate_row(g, 0) * o_c + gate_row(g, 2) * o_w

    def scores(g, qs_aug, it, m8, diagonal):
        k0 = pl.multiple_of(it * SWEEP, SWEEP)
        k_aug = jnp.concatenate([k_ref[0, pl.ds(k0, SWEEP), 0:LANES], oh_ref[pl.ds(k0, SWEEP), :]], axis=1)
        sc = _dot_nt(k_aug, qs_aug) * LOG2E
        if diagonal:
            sc = jnp.where((k0 + _iota((SWEEP, 1), 0)) <= t_row3, sc, -jnp.inf)
        s_buf[g, it] = sc
        return jnp.maximum(m8, jnp.max(sc.reshape(-1, 8, m), axis=0))

    def values(g, m_s, it, acc):
        pb = jnp.exp2(s_buf[g, it] - m_s).astype(BF16)
        for h in range(per_sweep):
            acc = acc + _dot(value_rows(g, 0, it * per_sweep + h), pb[h * KEY_TILE:(h + 1) * KEY_TILE])
        return acc

    pro = [prologue(g) for g in range(NSA_GROUPS)]
    qa = [pro[g][0] for g in range(NSA_GROUPS)]
    m8_init = jnp.full((8, m), M_FLOOR, F32)
    acc_init = jnp.zeros((HEAD_DIM + ONES_ROWS, m), F32)

    m8 = _sweep_loop(0, n_full, lambda it, c: scores(0, qa[0], it, c, False), m8_init)
    m8 = lax.fori_loop(n_full, n_it, lambda it, c: scores(0, qa[0], it, c, True), m8)
    m_s0 = jnp.max(m8, axis=0, keepdims=True)

    def fused(diagonal):
        return lambda it, c: (values(0, m_s0, it, c[0]), scores(1, qa[1], it, c[1], diagonal))

    acc0, m8 = _sweep_loop(0, n_full, fused(False), (acc_init, m8_init))
    acc0, m8 = lax.fori_loop(n_full, n_it, fused(True), (acc0, m8))
    m_s1 = jnp.max(m8, axis=0, keepdims=True)
    acc1 = _sweep_loop(0, n_it, lambda it, c: values(1, m_s1, it, c), acc_init)

    mix = [pro[g][1] + gate_row(g, 1) * (acc[0:HEAD_DIM] / acc[HEAD_DIM:HEAD_DIM + 1])
           for g, acc in enumerate((acc0, acc1))]
    for i in range(NSA_HPG):
        slab_t = jnp.concatenate([mix[g][:, i * tq:(i + 1) * tq] for g in range(NSA_GROUPS)], axis=0)
        o_ref[0, :, i * LANES:(i + 1) * LANES] = slab_t.T.astype(BF16)


def _rank(x):
    ns, tq = x.shape
    sub = _iota((8, 1), 0)
    parts = [x[r:r + 8] for r in range(0, ns, 8)]
    counts = [jnp.zeros((8, tq), F32) for _ in parts]
    for jp in range(ns):
        r = x[jp:jp + 1, :]
        for n, part in enumerate(parts):
            if 8 * n > jp:
                beats = r >= part
            elif 8 * n + 7 < jp:
                beats = r > part
            else:
                beats = (r > part) | ((r == part) & (sub + 8 * n > jp))
            counts[n] = counts[n] + jnp.where(beats, 1.0, 0.0)
    return jnp.concatenate(counts, axis=0)


def _nsa_attention(nq, gate_t, kc, vct, nk, vt, overlap_t, tq=256):
    b, s, _ = nq.shape
    nc = kc.shape[1]
    ns = overlap_t.shape[0]
    assert ns <= LANES
    block_of_key = np.arange(s)[:, None] // SEL_BLOCK == np.arange(LANES)[None, :]
    onehot = jnp.asarray(block_of_key.astype(np.float32), dtype=BF16)
    kern = functools.partial(_nsa_kernel, tq=tq, top=min(SEL_TOPK, ns))
    return pl.pallas_call(
        kern,
        grid=(b, s // tq),
        in_specs=[pl.BlockSpec((1, tq, NSA_D), lambda i, j: (i, j, 0)),
                  pl.BlockSpec((1, LANES, tq), lambda i, j: (i, 0, j)),
                  pl.BlockSpec((1, nc, LANES), lambda i, j: (i, 0, 0)),
                  pl.BlockSpec((1, LANES, nc), lambda i, j: (i, 0, 0)),
                  pl.BlockSpec((1, s, 2 * LANES), lambda i, j: (i, 0, 0)),
                  pl.BlockSpec((1, 2, s // KEY_TILE, LANES, KEY_TILE), lambda i, j: (i, 0, 0, 0, 0)),
                  pl.BlockSpec(overlap_t.shape, lambda i, j: (0, 0)),
                  pl.BlockSpec(onehot.shape, lambda i, j: (0, 0))],
        out_specs=pl.BlockSpec((1, tq, NSA_D), lambda i, j: (i, j, 0)),
        out_shape=jax.ShapeDtypeStruct((b, s, NSA_D), BF16),
        scratch_shapes=[pltpu.VMEM((NSA_GROUPS, s // SWEEP, SWEEP, NSA_HPG * tq), F32)],
        compiler_params=_cparams(("arbitrary", "arbitrary")),
        name="nsa_attention",
    )(nq, gate_t, kc, vct, nk, vt, overlap_t, onehot)


def _softplus(z):
    e = jnp.exp(-jnp.abs(z))
    return jnp.maximum(z, 0.0) + jnp.where(e < 2.0 ** -12, e, jnp.log(1.0 + e))


def _sb_kernel(q_ref, k_ref, vt_ref, o_ref, *, tq, n_heads):
    tk = KEY_TILE
    pair = pl.program_id(1)
    qi = pl.program_id(2)
    q = q_ref[0]
    lane = _iota((1, LANES), 1)
    t_row1 = qi * tq + _iota((1, tq), 1)
    row = _iota((LANES, 1), 0)
    later = jnp.where(_iota((tk, tk), 1) > _iota((tk, tk), 0), 1.0, 0.0).astype(BF16)

    def run(nh):
        qs = jnp.concatenate([jnp.where((lane // HEAD_DIM) == a, q, jnp.zeros((), BF16))
                              for a in range(nh)], axis=0)
        t_row = _lanes([t_row1] * nh)

        def tile(kt, run_row, acc, diagonal):
            k0 = pl.multiple_of(kt * tk, tk)
            z = _dot_nt(k_ref[0, pl.ds(k0, tk), :], qs)
            lneg = -_softplus(z)
            if diagonal:
                msk = (k0 + _iota((tk, 1), 0)) < t_row
                lm = jnp.where(msk, lneg, 0.0)
            else:
                lm = lneg
            hi, lo = _hi_lo(lm)
            between = _dot(later, hi) + _dot(later, lo) + run_row
            a_w = jnp.exp(lneg + z + between)
            if diagonal:
                a_w = jnp.where(msk, a_w, 0.0)
            acc = acc + _dot(vt_ref[0, 0, kt], a_w.astype(BF16))
            return run_row + jnp.sum(lm, axis=0, keepdims=True), acc

        def alive(c):
            return (c[0] >= 0) & (jnp.max(c[1]) > SB_DEAD)

        def step(c):
            r, a = tile(c[0], c[1], c[2], False)
            return c[0] - 1, r, a

        def emit(acc):
            if nh == 2:
                o_t = jnp.where(row < HEAD_DIM, acc[:, 0:tq], acc[:, tq:2 * tq])
            else:
                o_t = jnp.where(row < HEAD_DIM, acc, 0.0)
            o_ref[0] = o_t.T.astype(BF16)

        zeros = (jnp.zeros((1, nh * tq), F32), jnp.zeros((LANES, nh * tq), F32))

        @pl.when(qi == 0)
        def _():
            emit(tile(qi, *zeros, True)[1])

        @pl.when(qi > 0)
        def _():
            run_row, acc = tile(qi, *zeros, True)
            run_row, acc = tile(qi - 1, run_row, acc, False)
            emit(lax.while_loop(alive, step, (qi - 2, run_row, acc))[2])

    @pl.when(pair * 2 + 1 < n_heads)
    def _():
        run(2)

    @pl.when(pair * 2 + 1 >= n_heads)
    def _():
        run(1)


def _sb_attention(sb, vt):
    b, s, _ = sb.shape
    tq = KEY_TILE
    kern = functools.partial(_sb_kernel, tq=tq, n_heads=SB_HEADS)
    return pl.pallas_call(
        kern,
        grid=(b, PAIR_SLABS, s // tq),
        in_specs=[pl.BlockSpec((1, tq, LANES), lambda i, p, j: (i, j, p)),
                  pl.BlockSpec((1, s, LANES), lambda i, p, j: (i, 0, PAIR_SLABS + p)),
                  pl.BlockSpec((1, 1, s // KEY_TILE, LANES, KEY_TILE), lambda i, p, j: (i, 2 + p, 0, 0, 0))],
        out_specs=pl.BlockSpec((1, tq, LANES), lambda i, p, j: (i, j, p)),
        out_shape=jax.ShapeDtypeStruct((b, s, PAIR_D), BF16),
        compiler_params=_cparams(("arbitrary", "arbitrary", "arbitrary")),
        name="sb_attention",
    )(sb, sb, vt)


def _mla_kernel(qn_ref, qr_ref, kn_ref, kpe_ref, vt_ref, o_ref, s_buf, *, tq, n_heads):
    pair = pl.program_id(1)
    t0 = pl.program_id(2) * tq
    per_sweep = SWEEP // KEY_TILE
    q = jnp.concatenate([qn_ref[0], qr_ref[0]], axis=1)
    lane2 = _iota((1, 2 * LANES), 1)
    lane_head = jnp.where(lane2 < LANES, lane2 // MLA_NOPE, (lane2 - LANES) // MLA_ROPE)
    t_row1 = t0 + _iota((1, tq), 1)
    row = _iota((LANES, 1), 0)
    scale = (MLA_NOPE + MLA_ROPE) ** -0.5
    ones_rows = jnp.ones((ONES_ROWS, KEY_TILE), BF16)
    n_full = t0 // SWEEP
    n_it = (t0 + tq + SWEEP - 1) // SWEEP

    def run(nh):
        m = nh * tq
        qs = jnp.concatenate([jnp.where(lane_head == a, q, jnp.zeros((), BF16))
                              for a in range(nh)], axis=0)
        t_row = _lanes([t_row1] * nh)

        def scores(it, m8, diagonal):
            k0 = pl.multiple_of(it * SWEEP, SWEEP)
            k = jnp.concatenate([kn_ref[0, pl.ds(k0, SWEEP), :], kpe_ref[0, pl.ds(k0, SWEEP), :]], axis=1)
            sc = _dot_nt(k, qs) * (scale * LOG2E)
            if diagonal:
                sc = jnp.where((k0 + _iota((SWEEP, 1), 0)) <= t_row, sc, -jnp.inf)
            s_buf[it, :, 0:m] = sc
            return jnp.maximum(m8, jnp.max(sc.reshape(-1, 8, m), axis=0))

        m8 = _sweep_loop(0, n_full, lambda it, c: scores(it, c, False), jnp.full((8, m), M_FLOOR, F32))
        m8 = lax.fori_loop(n_full, n_it, lambda it, c: scores(it, c, True), m8)
        m_f = jnp.max(m8, axis=0, keepdims=True)

        def values(it, acc):
            pb = jnp.exp2(s_buf[it, :, 0:m] - m_f).astype(BF16)
            for h in range(per_sweep):
                v_aug = jnp.concatenate([vt_ref[0, 0, it * per_sweep + h], ones_rows], axis=0)
                acc = acc + _dot(v_aug, pb[h * KEY_TILE:(h + 1) * KEY_TILE])
            return acc

        acc = _sweep_loop(0, n_it, values, jnp.zeros((LANES + ONES_ROWS, m), F32))
        o = acc[0:LANES] / acc[LANES:LANES + 1]
        if nh == 2:
            o_t = jnp.where(row < MLA_V, o[:, 0:tq], o[:, tq:2 * tq])
        else:
            o_t = jnp.where(row < MLA_V, o, 0.0)
        o_ref[0] = o_t.T.astype(BF16)

    @pl.when(pair * 2 + 1 < n_heads)
    def _():
        run(2)

    @pl.when(pair * 2 + 1 >= n_heads)
    def _():
        run(1)


def _mla_attention(mla, vt, tq=SWEEP):
    b, s, _ = mla.shape
    kern = functools.partial(_mla_kernel, tq=tq, n_heads=MLA_HEADS)
    ps = PAIR_SLABS
    return pl.pallas_call(
        kern,
        grid=(b, ps, s // tq),
        in_specs=[pl.BlockSpec((1, tq, LANES), lambda i, p, j: (i, j, p)),
                  pl.BlockSpec((1, tq, LANES), lambda i, p, j: (i, j, ps + p)),
                  pl.BlockSpec((1, s, LANES), lambda i, p, j: (i, 0, 2 * ps + p)),
                  pl.BlockSpec((1, s, LANES), lambda i, p, j: (i, 0, 3 * ps)),
                  pl.BlockSpec((1, 1, s // KEY_TILE, LANES, KEY_TILE),
                               lambda i, p, j: (i, 2 + ps + p, 0, 0, 0))],
        out_specs=pl.BlockSpec((1, tq, LANES), lambda i, p, j: (i, j, p)),
        out_shape=jax.ShapeDtypeStruct((b, s, PAIR_D), BF16),
        scratch_shapes=[pltpu.VMEM((s // SWEEP, SWEEP, 2 * tq), F32)],
        compiler_params=_cparams(("arbitrary", "arbitrary", "arbitrary")),
        name="mla_attention",
    )(mla, mla, mla, mla, vt)


def _outproj_kernel(oa_ref, ob_ref, oc_ref, z_ref, x_ref, mod_ref, w_ref, fg_ref, o_ref, *, final):
    mix = (jnp.concatenate([oa_ref[0], ob_ref[0], oc_ref[0]], axis=1).astype(F32)
           * jax.nn.silu(z_ref[0].astype(F32)))
    y = _dot(mix.astype(BF16), w_ref[...])
    xn = x_ref[0] + mod_ref[0, 2:3, :] * y
    if final:
        ms = jnp.mean(xn * xn, axis=-1, keepdims=True)
        xn = xn * lax.rsqrt(ms + EPS) * fg_ref[...]
    o_ref[0] = xn


def _out_projection(oa, ob, oc, z, x, mod3, w_out_p, final_g, final, ts=256):
    b, s, d = x.shape
    row = lambda n: pl.BlockSpec((1, ts, n), lambda i, j: (i, j, 0))
    full = lambda a: pl.BlockSpec(a.shape, lambda i, j: (0,) * a.ndim)
    return pl.pallas_call(
        functools.partial(_outproj_kernel, final=final),
        grid=(b, s // ts),
        in_specs=[row(NSA_D), row(PAIR_D), row(PAIR_D), row(3 * PAIR_D), row(d),
                  pl.BlockSpec((1, 3, d), lambda i, j: (i, 0, 0)), full(w_out_p), full(final_g)],
        out_specs=row(d),
        out_shape=jax.ShapeDtypeStruct((b, s, d), F32),
        compiler_params=_cparams(("arbitrary", "arbitrary")),
        name="out_projection",
    )(oa, ob, oc, z, x, mod3, w_out_p, final_g)


def _overlap_matrix_t(s):
    n_cmp = (s - CMP_LEN) // CMP_STRIDE + 1
    n_sel = s // SEL_BLOCK
    c_start = np.arange(s // CMP_STRIDE) * CMP_STRIDE
    s_start = np.arange(n_sel) * SEL_BLOCK
    ov = ((c_start[None, :] < s_start[:, None] + SEL_BLOCK)
          & (c_start[None, :] + CMP_LEN > s_start[:, None])
          & (np.arange(s // CMP_STRIDE)[None, :] < n_cmp))
    return jnp.asarray(ov.astype(np.float32), dtype=BF16)


def _mixer_layer(x, mod3, tabs, overlap_t, norm_g, w_in, pos_k, pos_v, ck_w1, ck_w2, cv_w1, cv_w2,
                 q_norm, w_uq, kv_norm, w_ukv, w_out, final_g, final):
    b, s, d = x.shape
    w_in_p = _take_cols(w_in, _in_cols()).astype(BF16)
    w_uq_p = jnp.pad(_take_cols(w_uq, _uq_cols()), ((0, Q_RANK_PAD - MLA_Q_RANK), (0, 0))).astype(BF16)
    w_ukv_p = _take_cols(w_ukv, _ukv_cols()).astype(BF16)
    q_norm_p = jnp.pad(q_norm, (0, Q_RANK_PAD - MLA_Q_RANK)).reshape(1, Q_RANK_PAD)
    w_out_p = _take_cols(w_out, _out_rows(), axis=0).astype(BF16)

    nq, nk, cmp_tok, gate_t, z, sb, mla, vt = _in_projection(
        x, mod3, norm_g.reshape(1, d), w_in_p, q_norm_p, w_uq_p, kv_norm.reshape(1, -1), w_ukv_p, tabs)

    chunk_w = CMP_STRIDE * LANES
    kc_chunks = cmp_tok[:, :, 0:LANES].reshape(b, s // CMP_STRIDE, chunk_w)
    vc_chunks = cmp_tok[:, :, LANES:].reshape(b, s // CMP_STRIDE, chunk_w)
    pos4 = jnp.concatenate([_cmp_pos(pos_k), _cmp_pos(pos_v)], axis=0)
    kc, vct = _nsa_compress(kc_chunks, vc_chunks, pos4,
                            _cmp_weights(ck_w1, ck_w2), _cmp_weights(cv_w1, cv_w2))

    o_a = _nsa_attention(nq, gate_t, kc, vct, nk, vt, overlap_t)
    o_b = _sb_attention(sb, vt)
    o_c = _mla_attention(mla, vt)
    return _out_projection(o_a, o_b, o_c, z, x, mod3, w_out_p, final_g.reshape(1, d), final)


def kernel(x, c, positions, ada_w, ada_b, norm_g, w_in, nsa_pos_k, nsa_pos_v, nsa_ck_w1, nsa_ck_w2,
           nsa_cv_w1, nsa_cv_w2, mla_q_norm, mla_w_uq, mla_kv_norm, mla_w_ukv, w_out, final_norm):
    b, s, d = x.shape
    depth = w_in.shape[0]
    assert s % SWEEP == 0 and s >= WINDOW + 2 * KEY_TILE
    mod = _adaln_mod(c, ada_w, ada_b).reshape(depth, b, 3, d)
    tabs = _rope_tables(positions)
    overlap_t = _overlap_matrix_t(s)
    for l in range(depth):
        x = _mixer_layer(x, mod[l], tabs, overlap_t, norm_g[l], w_in[l], nsa_pos_k[l], nsa_pos_v[l],
                         nsa_ck_w1[l], nsa_ck_w2[l], nsa_cv_w1[l], nsa_cv_w2[l],
                         mla_q_norm[l], mla_w_uq[l], mla_kv_norm[l], mla_w_ukv[l], w_out[l],
                         final_norm, final=(l == depth - 1))
    return x
```

```python
import functools

import numpy as np
import jax
import jax.numpy as jnp
from jax import lax
from jax.experimental import pallas as pl
from jax.experimental.pallas import tpu as pltpu

F32 = jnp.float32
BF16 = jnp.bfloat16

LANES = 128
HEAD_DIM = 64
NSA_HEADS = 6
NSA_GROUPS = 2
NSA_HPG = NSA_HEADS // NSA_GROUPS
SB_HEADS = 5
MLA_HEADS = 5
MLA_NOPE = 64
MLA_ROPE = 32
MLA_V = 64
MLA_Q_RANK = 192
MLA_KV_RANK = 128
CMP_LEN = 32
CMP_STRIDE = 16
CMP_HIDDEN = 128
SEL_BLOCK = 64
SEL_TOPK = 16
WINDOW = 512
ROPE_THETA = 10000.0
EPS = 1e-6
SEL_FORCE = 1e9
M_FLOOR = -1e30
SB_DEAD = -110.0
LOG2E = 1.4426950408889634
NEG_MASK = -1e30
ONES_ROWS = 16

NSA_D = NSA_HEADS * HEAD_DIM
PAIR_SLABS = 3
PAIR_D = PAIR_SLABS * LANES
Q_RANK_PAD = 256

KEY_TILE = 256
SWEEP = 2 * KEY_TILE
VT_SLABS = 8

C_NQ = 0
C_NKV = C_NQ + NSA_D
C_CMP = C_NKV + 4 * LANES
C_GATE = C_CMP + 2 * LANES
C_Z = C_GATE + LANES
C_SB = C_Z + 3 * PAIR_D
C_CQ = C_SB + 3 * PAIR_D
C_CKV = C_CQ + Q_RANK_PAD
C_KR = C_CKV + LANES
D_IN_PAD = C_KR + LANES

VMEM_LIMIT = 52 * 1024 * 1024


def _cparams(sem):
    return pltpu.CompilerParams(dimension_semantics=sem, vmem_limit_bytes=VMEM_LIMIT)


def _dot(a, b):
    return jnp.dot(a, b, preferred_element_type=F32)


def _dot_nt(a, b):
    return lax.dot_general(a, b, (((1,), (1,)), ((), ())), preferred_element_type=F32)


def _iota(shape, dim):
    return lax.broadcasted_iota(jnp.int32, shape, dim)


def _rope(x, cos, sin_signed, width):
    half = width // 2
    lane = _iota(x.shape, 1)
    first = (lane % width) < half
    rot = jnp.where(first, pltpu.roll(x, LANES - half, 1), pltpu.roll(x, half, 1))
    return x * cos + rot * sin_signed


def _hi_lo(x):
    hi = x.astype(BF16)
    return hi, (x - hi.astype(F32)).astype(BF16)


def _lanes(parts):
    return parts[0] if len(parts) == 1 else jnp.concatenate(parts, axis=1)


def _sweep_loop(lo, hi, step, carry):
    def block(width):
        def body(i, c):
            for u in range(width):
                c = step(i + u, c)
            return c
        return body

    for width in (4, 2):
        n = (hi - lo) // width
        carry = lax.fori_loop(0, n, lambda i, c, w=width, base=lo: block(w)(base + w * i, c), carry)
        lo = lo + width * n
    return lax.fori_loop(lo, hi, step, carry)


def _in_cols():
    widths = (NSA_D, 128, 128, 128, 128, 128, 128, NSA_HEADS * 3, NSA_D,
              320, 320, 320, 320, MLA_Q_RANK, MLA_KV_RANK, MLA_ROPE, 320)
    off = np.concatenate([[0], np.cumsum(widths)])
    (o_nq, o_kc, o_vc, o_ks, o_vs, o_kw, o_vw, o_gate, o_nz,
     o_sq, o_sk, o_sv, o_sz, o_cq, o_ckv, o_kr, o_mz) = [int(v) for v in off[:-1]]
    cols = -np.ones((D_IN_PAD,), np.int64)

    def put(dst, src, n):
        cols[dst:dst + n] = src + np.arange(n)

    def put_nsa(dst, src):
        for i in range(NSA_HPG):
            for g in range(NSA_GROUPS):
                put(dst + i * LANES + g * HEAD_DIM, src + (g * NSA_HPG + i) * HEAD_DIM, HEAD_DIM)

    put_nsa(C_NQ, o_nq)
    put(C_NKV, o_ks, 128)
    put(C_NKV + 128, o_vs, 128)
    put(C_NKV + 256, o_kw, 128)
    put(C_NKV + 384, o_vw, 128)
    put(C_CMP, o_kc, 128)
    put(C_CMP + 128, o_vc, 128)
    put(C_GATE, o_gate, NSA_HEADS * 3)
    put_nsa(C_Z, o_nz)
    put(C_Z + PAIR_D, o_sz, 320)
    put(C_Z + 2 * PAIR_D, o_mz, 320)
    put(C_SB, o_sq, 320)
    put(C_SB + PAIR_D, o_sk, 320)
    put(C_SB + 2 * PAIR_D, o_sv, 320)
    put(C_CQ, o_cq, MLA_Q_RANK)
    put(C_CKV, o_ckv, MLA_KV_RANK)
    put(C_KR, o_kr, MLA_ROPE)
    put(C_KR + MLA_ROPE, o_kr, MLA_ROPE)
    return cols


def _take_cols(w, cols, axis=1):
    cols = np.asarray(cols)
    g = jnp.take(w, jnp.asarray(np.maximum(cols, 0)), axis=axis)
    keep = jnp.asarray(cols >= 0).reshape([-1 if a == axis else 1 for a in range(w.ndim)])
    return jnp.where(keep, g, 0.0)


def _uq_cols():
    cols = -np.ones((2 * PAIR_D,), np.int64)
    for h in range(MLA_HEADS):
        p, a = divmod(h, 2)
        base = h * (MLA_NOPE + MLA_ROPE)
        cols[p * LANES + a * MLA_NOPE: p * LANES + (a + 1) * MLA_NOPE] = base + np.arange(MLA_NOPE)
        d = PAIR_D + p * LANES + a * MLA_ROPE
        cols[d: d + MLA_ROPE] = base + MLA_NOPE + np.arange(MLA_ROPE)
    return cols


def _ukv_cols():
    cols = -np.ones((2 * PAIR_D,), np.int64)
    for h in range(MLA_HEADS):
        base = h * (MLA_NOPE + MLA_V)
        cols[h * MLA_NOPE: (h + 1) * MLA_NOPE] = base + np.arange(MLA_NOPE)
        cols[PAIR_D + h * MLA_V: PAIR_D + (h + 1) * MLA_V] = base + MLA_NOPE + np.arange(MLA_V)
    return cols


def _out_rows():
    rows = -np.ones((3 * PAIR_D,), np.int64)
    for i in range(NSA_HPG):
        for g in range(NSA_GROUPS):
            d = i * LANES + g * HEAD_DIM
            rows[d:d + HEAD_DIM] = (g * NSA_HPG + i) * HEAD_DIM + np.arange(HEAD_DIM)
    rows[PAIR_D:PAIR_D + 320] = NSA_D + np.arange(320)
    rows[2 * PAIR_D:2 * PAIR_D + 320] = NSA_D + 320 + np.arange(320)
    return rows


def _cmp_weights(w1, w2):
    half = CMP_LEN // 2
    w1r = w1.reshape(CMP_LEN, HEAD_DIM, CMP_HIDDEN)
    z = jnp.zeros((half, HEAD_DIM, CMP_HIDDEN), w1.dtype)

    def build(part):
        g0 = jnp.concatenate([part, z], axis=1)
        g1 = jnp.concatenate([z, part], axis=1)
        return jnp.concatenate([g0.reshape(half * LANES, CMP_HIDDEN),
                                g1.reshape(half * LANES, CMP_HIDDEN)], axis=1)

    zz = jnp.zeros_like(w2)
    w2bd = jnp.concatenate([jnp.concatenate([w2, zz], axis=1),
                            jnp.concatenate([zz, w2], axis=1)], axis=0)
    return build(w1r[:half]).astype(BF16), build(w1r[half:]).astype(BF16), w2bd.astype(BF16)


def _cmp_pos(pos):
    half = CMP_LEN // 2
    tiled = jnp.concatenate([pos, pos], axis=1)
    return jnp.stack([tiled[:half].reshape(half * LANES), tiled[half:].reshape(half * LANES)])


def _mod_kernel(c_ref, w_ref, b_ref, o_ref):
    a = jax.nn.silu(c_ref[...])
    o_ref[0] = _dot(a, w_ref[0]) + b_ref[0]


def _adaln_mod(c, ada_w, ada_b):
    depth, d, n = ada_w.shape
    b = c.shape[0]
    tn = 1024
    return pl.pallas_call(
        _mod_kernel,
        grid=(depth, n // tn),
        in_specs=[pl.BlockSpec((b, d), lambda l, j: (0, 0)),
                  pl.BlockSpec((1, d, tn), lambda l, j: (l, 0, j)),
                  pl.BlockSpec((1, 1, tn), lambda l, j: (l, 0, j))],
        out_specs=pl.BlockSpec((1, b, tn), lambda l, j: (l, 0, j)),
        out_shape=jax.ShapeDtypeStruct((depth, b, n), F32),
        compiler_params=_cparams(("arbitrary", "arbitrary")),
        name="adaln_mod",
    )(c, ada_w, ada_b.reshape(depth, 1, n))


def _rope_kernel(pos_ref, inv_ref, c64_ref, s64_ref, c32_ref, s32_ref):
    pos = pos_ref[0].astype(F32)
    rows = pos.shape[0] // 2
    lane = _iota((1, LANES), 1)
    ang = jnp.where(lane < LANES // 2, pos[0:rows], pos[rows:]) * inv_ref[0:1, :]
    n64, n32 = HEAD_DIM // 2, MLA_ROPE // 2

    def tile_lanes(x, first, period):
        y = jnp.where((lane >= first) & (lane < first + period), x, 0.0)
        if first:
            y = pltpu.roll(y, LANES - first, 1)
        width = period
        while width < LANES:
            y = y + pltpu.roll(y, width, 1)
            width *= 2
        return y

    cos, sin = jnp.cos(ang), jnp.sin(ang)
    for half, base in enumerate((0, LANES // 2)):
        out = slice(half * rows, (half + 1) * rows)
        c64_ref[0, out] = tile_lanes(cos, base, n64)
        s64 = tile_lanes(sin, base, n64)
        s64_ref[0, out] = jnp.where((lane % HEAD_DIM) < n64, -s64, s64)
        c32_ref[0, out] = tile_lanes(cos, base + n64, n32)
        s32 = tile_lanes(sin, base + n64, n32)
        s32_ref[0, out] = jnp.where((lane % MLA_ROPE) < n32, -s32, s32)


def _rope_tables(positions):
    b, s = positions.shape
    ts = 512

    def inv(half):
        return ROPE_THETA ** (-jnp.arange(half, dtype=F32) / half)

    packed = jnp.concatenate([inv(HEAD_DIM // 2), inv(MLA_ROPE // 2)])
    inv_tab = (jnp.zeros((8, LANES), F32).at[0, 0:packed.shape[0]].set(packed)
               .at[0, LANES // 2:LANES // 2 + packed.shape[0]].set(packed))
    tab = jax.ShapeDtypeStruct((b, s, LANES), F32)
    spec = pl.BlockSpec((1, ts, LANES), lambda i, j: (i, j, 0))
    return pl.pallas_call(
        _rope_kernel,
        grid=(b, s // ts),
        in_specs=[pl.BlockSpec((1, ts, 1), lambda i, j: (i, j, 0)),
                  pl.BlockSpec((8, LANES), lambda i, j: (0, 0))],
        out_specs=[spec] * 4,
        out_shape=[tab] * 4,
        compiler_params=_cparams(("arbitrary", "arbitrary")),
        name="rope_tables",
    )(positions.reshape(b, s, 1), inv_tab)


def _inproj_kernel(x_ref, mod_ref, g_ref, w_ref, qn_ref, wuq_ref, kvn_ref, wukv_ref,
                   c64_ref, s64_ref, c32_ref, s32_ref,
                   nq_ref, nk_ref, cmp_ref, gatet_ref, z_ref, sb_ref, mla_ref, vt_ref):
    x = x_ref[0]
    ms = jnp.mean(x * x, axis=-1, keepdims=True)
    y = x * lax.rsqrt(ms + EPS) * g_ref[...]
    h = y * (1.0 + mod_ref[0, 1:2, :]) + mod_ref[0, 0:1, :]
    hb = h.astype(BF16)

    def proj(c0, n):
        return _dot(hb, w_ref[:, c0:c0 + n])

    def slab(val, c0, j):
        return val[:, c0 + j * LANES:c0 + (j + 1) * LANES]

    def put_vt(slab_id, val):
        for sub in range(val.shape[0] // KEY_TILE):
            vt_ref[0, slab_id, sub] = val[sub * KEY_TILE:(sub + 1) * KEY_TILE].T.astype(BF16)

    c64, s64 = c64_ref[0], s64_ref[0]
    c32, s32 = c32_ref[0], s32_ref[0]
    q_scale = HEAD_DIM ** -0.5

    nsa = proj(C_NQ, C_Z - C_NQ)
    for i in range(NSA_HPG):
        q = _rope(slab(nsa, C_NQ, i), c64, s64, HEAD_DIM)
        nq_ref[0, :, i * LANES:(i + 1) * LANES] = (q * q_scale).astype(BF16)
    nk_ref[0, :, 0:LANES] = _rope(slab(nsa, C_NKV, 0), c64, s64, HEAD_DIM).astype(BF16)
    put_vt(0, slab(nsa, C_NKV, 1))
    nk_ref[0, :, LANES:2 * LANES] = _rope(slab(nsa, C_NKV, 2), c64, s64, HEAD_DIM).astype(BF16)
    put_vt(1, slab(nsa, C_NKV, 3))
    cmp_ref[0, :, 0:LANES] = _rope(slab(nsa, C_CMP, 0), c64, s64, HEAD_DIM)
    cmp_ref[0, :, LANES:2 * LANES] = slab(nsa, C_CMP, 1)
    gatet_ref[0] = jax.nn.sigmoid(slab(nsa, C_GATE, 0)).T
    z_ref[0] = proj(C_Z, 3 * PAIR_D).astype(BF16)
    sbp = proj(C_SB, 3 * PAIR_D)
    sb_ref[0, :, 0:PAIR_D] = (sbp[:, 0:PAIR_D] * q_scale).astype(BF16)
    sb_ref[0, :, PAIR_D:2 * PAIR_D] = sbp[:, PAIR_D:2 * PAIR_D].astype(BF16)
    for p in range(PAIR_SLABS):
        put_vt(2 + p, slab(sbp, 2 * PAIR_D, p))

    lat = proj(C_CQ, D_IN_PAD - C_CQ)
    cq = lat[:, 0:Q_RANK_PAD]
    cqn = cq * lax.rsqrt(jnp.sum(cq * cq, axis=-1, keepdims=True) / MLA_Q_RANK + EPS) * qn_ref[...]
    q2 = _dot(cqn.astype(BF16), wuq_ref[...])
    mla_ref[0, :, 0:PAIR_D] = q2[:, 0:PAIR_D].astype(BF16)
    for p in range(PAIR_SLABS):
        r = _rope(q2[:, PAIR_D + p * LANES:PAIR_D + (p + 1) * LANES], c32, s32, MLA_ROPE)
        mla_ref[0, :, PAIR_D + p * LANES:PAIR_D + (p + 1) * LANES] = r.astype(BF16)
    ckv = lat[:, Q_RANK_PAD:Q_RANK_PAD + LANES]
    ckvn = ckv * lax.rsqrt(jnp.mean(ckv * ckv, axis=-1, keepdims=True) + EPS) * kvn_ref[...]
    kv2 = _dot(ckvn.astype(BF16), wukv_ref[...])
    mla_ref[0, :, 2 * PAIR_D:3 * PAIR_D] = kv2[:, 0:PAIR_D].astype(BF16)
    for p in range(PAIR_SLABS):
        put_vt(2 + PAIR_SLABS + p, slab(kv2, PAIR_D, p))
    kr = _rope(lat[:, Q_RANK_PAD + LANES:Q_RANK_PAD + 2 * LANES], c32, s32, MLA_ROPE)
    mla_ref[0, :, 3 * PAIR_D:3 * PAIR_D + LANES] = kr.astype(BF16)


def _in_projection(x, mod3, norm_g, w_in_p, q_norm_p, w_uq_p, kv_norm, w_ukv_p, tabs):
    b, s, d = x.shape
    ts = 2 * KEY_TILE
    row = lambda n: pl.BlockSpec((1, ts, n), lambda i, j: (i, j, 0))
    full = lambda a: pl.BlockSpec(a.shape, lambda i, j: (0,) * a.ndim, pipeline_mode=pl.Buffered(1))
    outs = [(NSA_D, BF16), (2 * LANES, BF16), (2 * LANES, F32), None,
            (3 * PAIR_D, BF16), (2 * PAIR_D, BF16), (3 * PAIR_D + LANES, BF16), None]
    out_specs = [row(o[0]) if o else None for o in outs]
    out_shape = [jax.ShapeDtypeStruct((b, s, o[0]), o[1]) if o else None for o in outs]
    out_specs[3] = pl.BlockSpec((1, LANES, ts), lambda i, j: (i, 0, j))
    out_shape[3] = jax.ShapeDtypeStruct((b, LANES, s), F32)
    out_specs[7] = pl.BlockSpec((1, VT_SLABS, ts // KEY_TILE, LANES, KEY_TILE), lambda i, j: (i, 0, j, 0, 0))
    out_shape[7] = jax.ShapeDtypeStruct((b, VT_SLABS, s // KEY_TILE, LANES, KEY_TILE), BF16)
    return pl.pallas_call(
        _inproj_kernel,
        grid=(b, s // ts),
        in_specs=[row(d), pl.BlockSpec((1, 3, d), lambda i, j: (i, 0, 0)), full(norm_g),
                  full(w_in_p), full(q_norm_p), full(w_uq_p), full(kv_norm), full(w_ukv_p),
                  row(LANES), row(LANES), row(LANES), row(LANES)],
        out_specs=out_specs,
        out_shape=out_shape,
        compiler_params=_cparams(("arbitrary", "arbitrary")),
        name="in_projection",
    )(x, mod3, norm_g, w_in_p, q_norm_p, w_uq_p, kv_norm, w_ukv_p, *tabs)


def _cmp_kernel(ktok_ref, vtok_ref, pos_ref, wkt_ref, wkb_ref, wk2_ref, wvt_ref, wvb_ref, wv2_ref,
                kc_out, vct_out):
    nc = kc_out.shape[1]

    def compress(tok_ref, pos_row, wt_ref, wb_ref, w2_ref):
        a = jnp.zeros((nc, 2 * CMP_HIDDEN), F32)
        bm = jnp.zeros((nc, 2 * CMP_HIDDEN), F32)
        for i in range(CMP_STRIDE):
            tok = tok_ref[0, pl.ds(i, nc, stride=CMP_STRIDE), :]
            w = slice(i * LANES, (i + 1) * LANES)
            a = a + _dot((tok + pos_ref[pos_row:pos_row + 1, w]).astype(BF16), wt_ref[w, :])
            bm = bm + _dot((tok + pos_ref[pos_row + 1:pos_row + 2, w]).astype(BF16), wb_ref[w, :])
        pre = a + pltpu.roll(bm, nc - 1, 0)
        return _dot(jax.nn.silu(pre).astype(BF16), w2_ref[...])

    kc_out[0] = compress(ktok_ref, 0, wkt_ref, wkb_ref, wk2_ref).astype(BF16)
    vct_out[0] = compress(vtok_ref, 2, wvt_ref, wvb_ref, wv2_ref).T.astype(BF16)


def _nsa_compress(cmp_tok, pos4, wk, wv):
    b, s, cw = cmp_tok.shape
    nc = s // CMP_STRIDE
    full = lambda a: pl.BlockSpec(a.shape, lambda i: (0,) * a.ndim)
    return pl.pallas_call(
        _cmp_kernel,
        grid=(b,),
        in_specs=[pl.BlockSpec((1, s, LANES), lambda i: (i, 0, 0)),
                  pl.BlockSpec((1, s, LANES), lambda i: (i, 0, 1)), full(pos4)]
        + [full(w) for w in wk] + [full(w) for w in wv],
        out_specs=[pl.BlockSpec((1, nc, LANES), lambda i: (i, 0, 0)),
                   pl.BlockSpec((1, LANES, nc), lambda i: (i, 0, 0))],
        out_shape=[jax.ShapeDtypeStruct((b, nc, LANES), BF16),
                   jax.ShapeDtypeStruct((b, LANES, nc), BF16)],
        compiler_params=_cparams(("arbitrary",)),
        name="nsa_compress",
    )(cmp_tok, cmp_tok, pos4, *wk, *wv)


def _nsa_kernel(q_ref, gt_ref, kc_ref, vct_ref, k_ref, vt_ref, ovt_ref, oh_ref, o_ref, s_buf,
                *, tq, top):
    t0 = pl.program_id(1) * tq
    nc = kc_ref.shape[1]
    ns = ovt_ref.shape[0]
    per_sweep = SWEEP // KEY_TILE
    win_tiles = -(-(WINDOW + tq + max(KEY_TILE - tq, 0)) // KEY_TILE)
    ones_rows = jnp.ones((ONES_ROWS, KEY_TILE), BF16)
    m = NSA_HPG * tq
    q = q_ref[0]
    gt = gt_ref[0]
    lane = _iota((1, LANES), 1)
    t_row = t0 + _iota((1, tq), 1)

    def heads(a):
        return _lanes([a] * NSA_HPG)

    t_row3 = heads(t_row)
    vis_bias = jnp.where((_iota((nc, 1), 0) * CMP_STRIDE + (CMP_LEN - 1)) <= t_row, 0.0, -jnp.inf)
    j_col = _iota((ns, 1), 0)
    cur = t_row // SEL_BLOCK
    forced = (j_col == 0) | (j_col == cur) | (j_col == cur - 1)
    valid = j_col * SEL_BLOCK <= t_row
    n_full = t0 // SWEEP
    n_it = (t0 + tq + SWEEP - 1) // SWEEP
    wt = jnp.maximum(t0 - WINDOW, 0) // KEY_TILE
    w0 = pl.multiple_of(wt * KEY_TILE, KEY_TILE)
    wlen = win_tiles * KEY_TILE
    tokw = w0 + _iota((wlen, 1), 0)
    win_bias = jnp.where((tokw <= t_row) & (tokw > t_row - WINDOW), 0.0, -jnp.inf)

    def gate_row(g, br):
        return _lanes([gt[(g * NSA_HPG + i) * 3 + br:(g * NSA_HPG + i) * 3 + br + 1, :]
                       for i in range(NSA_HPG)])

    def value_rows(g, slab_id, kt):
        return jnp.concatenate([vt_ref[0, slab_id, kt, g * HEAD_DIM:(g + 1) * HEAD_DIM, :], ones_rows],
                               axis=0)

    def prologue(g):
        qs = jnp.concatenate([jnp.where((lane // HEAD_DIM) == g, q[:, i * LANES:(i + 1) * LANES],
                                        jnp.zeros((), BF16)) for i in range(NSA_HPG)], axis=0)

        sw = _dot_nt(k_ref[0, pl.ds(w0, wlen), LANES:2 * LANES], qs) + heads(win_bias)
        pwb = jnp.exp(sw - jnp.max(sw, axis=0, keepdims=True)).astype(BF16)
        acc_w = jnp.zeros((HEAD_DIM + ONES_ROWS, m), F32)
        for h in range(win_tiles):
            acc_w = acc_w + _dot(value_rows(g, 1, wt + h), pwb[h * KEY_TILE:(h + 1) * KEY_TILE])
        o_w = acc_w[0:HEAD_DIM] / acc_w[HEAD_DIM:HEAD_DIM + 1]

        s = _dot_nt(kc_ref[0], qs) + heads(vis_bias)
        mx = jnp.maximum(jnp.max(s, axis=0, keepdims=True), M_FLOOR)
        p = jnp.exp(s - mx)
        l = jnp.sum(p, axis=0, keepdims=True)
        pc = p * jnp.where(l > 0.0, 1.0 / l, 0.0)
        o_c = _dot(vct_ref[0, g * HEAD_DIM:(g + 1) * HEAD_DIM, :], pc.astype(BF16))

        hi, lo = _hi_lo(pc[:, 0:tq] + pc[:, tq:2 * tq] + pc[:, 2 * tq:3 * tq])
        imp_sel = _dot(ovt_ref[...], hi) + _dot(ovt_ref[...], lo)
        rank = _rank(jnp.where(valid, jnp.where(forced, SEL_FORCE, imp_sel), -SEL_FORCE))

        member = jnp.concatenate([jnp.where(rank < top, 0.0, NEG_MASK), jnp.zeros((LANES - ns, tq), F32)],
                                 axis=0).T.astype(BF16)
        qs_aug = jnp.concatenate([qs, jnp.concatenate([member] * NSA_HPG, axis=0)], axis=1)
        return qs_aug, gate_row(g, 0) * o_c + gate_row(g, 2) * o_w

    def scores(g, qs_aug, it, m8, diagonal):
        k0 = pl.multiple_of(it * SWEEP, SWEEP)
        k_aug = jnp.concatenate([k_ref[0, pl.ds(k0, SWEEP), 0:LANES], oh_ref[pl.ds(k0, SWEEP), :]], axis=1)
        sc = _dot_nt(k_aug, qs_aug) * LOG2E
        if diagonal:
            sc = jnp.where((k0 + _iota((SWEEP, 1), 0)) <= t_row3, sc, -jnp.inf)
        s_buf[g, it] = sc
        return jnp.maximum(m8, jnp.max(sc.reshape(-1, 8, m), axis=0))

    def values(g, m_s, it, acc):
        pb = jnp.exp2(s_buf[g, it] - m_s).astype(BF16)
        for h in range(per_sweep):
            acc = acc + _dot(value_rows(g, 0, it * per_sweep + h), pb[h * KEY_TILE:(h + 1) * KEY_TILE])
        return acc

    pro = [prologue(g) for g in range(NSA_GROUPS)]
    qa = [pro[g][0] for g in range(NSA_GROUPS)]
    m8_init = jnp.full((8, m), M_FLOOR, F32)
    acc_init = jnp.zeros((HEAD_DIM + ONES_ROWS, m), F32)

    m8 = _sweep_loop(0, n_full, lambda it, c: scores(0, qa[0], it, c, False), m8_init)
    m8 = lax.fori_loop(n_full, n_it, lambda it, c: scores(0, qa[0], it, c, True), m8)
    m_s0 = jnp.max(m8, axis=0, keepdims=True)

    def fused(diagonal):
        return lambda it, c: (values(0, m_s0, it, c[0]), scores(1, qa[1], it, c[1], diagonal))

    acc0, m8 = _sweep_loop(0, n_full, fused(False), (acc_init, m8_init))
    acc0, m8 = lax.fori_loop(n_full, n_it, fused(True), (acc0, m8))
    m_s1 = jnp.max(m8, axis=0, keepdims=True)
    acc1 = _sweep_loop(0, n_it, lambda it, c: values(1, m_s1, it, c), acc_init)

    mix = [pro[g][1] + gate_row(g, 1) * (acc[0:HEAD_DIM] / acc[HEAD_DIM:HEAD_DIM + 1])
           for g, acc in enumerate((acc0, acc1))]
    for i in range(NSA_HPG):
        slab_t = jnp.concatenate([mix[g][:, i * tq:(i + 1) * tq] for g in range(NSA_GROUPS)], axis=0)
        o_ref[0, :, i * LANES:(i + 1) * LANES] = slab_t.T.astype(BF16)


def _rank(x):
    ns, tq = x.shape
    sub = _iota((8, 1), 0)
    parts = [x[r:r + 8] for r in range(0, ns, 8)]
    counts = [jnp.zeros((8, tq), F32) for _ in parts]
    for jp in range(ns):
        r = x[jp:jp + 1, :]
        for n, part in enumerate(parts):
            if 8 * n > jp:
                beats = r >= part
            elif 8 * n + 7 < jp:
                beats = r > part
            else:
                beats = (r > part) | ((r == part) & (sub + 8 * n > jp))
            counts[n] = counts[n] + jnp.where(beats, 1.0, 0.0)
    return jnp.concatenate(counts, axis=0)


def _nsa_attention(nq, gate_t, kc, vct, nk, vt, overlap_t, tq=256):
    b, s, _ = nq.shape
    nc = kc.shape[1]
    ns = overlap_t.shape[0]
    assert ns <= LANES
    block_of_key = np.arange(s)[:, None] // SEL_BLOCK == np.arange(LANES)[None, :]
    onehot = jnp.asarray(block_of_key.astype(np.float32), dtype=BF16)
    kern = functools.partial(_nsa_kernel, tq=tq, top=min(SEL_TOPK, ns))
    return pl.pallas_call(
        kern,
        grid=(b, s // tq),
        in_specs=[pl.BlockSpec((1, tq, NSA_D), lambda i, j: (i, j, 0)),
                  pl.BlockSpec((1, LANES, tq), lambda i, j: (i, 0, j)),
                  pl.BlockSpec((1, nc, LANES), lambda i, j: (i, 0, 0)),
                  pl.BlockSpec((1, LANES, nc), lambda i, j: (i, 0, 0)),
                  pl.BlockSpec((1, s, 2 * LANES), lambda i, j: (i, 0, 0)),
                  pl.BlockSpec((1, 2, s // KEY_TILE, LANES, KEY_TILE), lambda i, j: (i, 0, 0, 0, 0)),
                  pl.BlockSpec(overlap_t.shape, lambda i, j: (0, 0)),
                  pl.BlockSpec(onehot.shape, lambda i, j: (0, 0))],
        out_specs=pl.BlockSpec((1, tq, NSA_D), lambda i, j: (i, j, 0)),
        out_shape=jax.ShapeDtypeStruct((b, s, NSA_D), BF16),
        scratch_shapes=[pltpu.VMEM((NSA_GROUPS, s // SWEEP, SWEEP, NSA_HPG * tq), F32)],
        compiler_params=_cparams(("arbitrary", "arbitrary")),
        name="nsa_attention",
    )(nq, gate_t, kc, vct, nk, vt, overlap_t, onehot)


def _softplus(z):
    e = jnp.exp(-jnp.abs(z))
    return jnp.maximum(z, 0.0) + jnp.where(e < 2.0 ** -12, e, jnp.log(1.0 + e))


def _sb_kernel(q_ref, k_ref, vt_ref, o_ref, *, tq, n_heads):
    tk = KEY_TILE
    pair = pl.program_id(1)
    qi = pl.program_id(2)
    q = q_ref[0]
    lane = _iota((1, LANES), 1)
    t_row1 = qi * tq + _iota((1, tq), 1)
    row = _iota((LANES, 1), 0)
    later = jnp.where(_iota((tk, tk), 1) > _iota((tk, tk), 0), 1.0, 0.0).astype(BF16)

    def run(nh):
        qs = jnp.concatenate([jnp.where((lane // HEAD_DIM) == a, q, jnp.zeros((), BF16))
                              for a in range(nh)], axis=0)
        t_row = _lanes([t_row1] * nh)

        def tile(kt, run_row, acc, diagonal):
            k0 = pl.multiple_of(kt * tk, tk)
            z = _dot_nt(k_ref[0, pl.ds(k0, tk), :], qs)
            lneg = -_softplus(z)
            if diagonal:
                msk = (k0 + _iota((tk, 1), 0)) < t_row
                lm = jnp.where(msk, lneg, 0.0)
            else:
                lm = lneg
            hi, lo = _hi_lo(lm)
            between = _dot(later, hi) + _dot(later, lo) + run_row
            a_w = jnp.exp(lneg + z + between)
            if diagonal:
                a_w = jnp.where(msk, a_w, 0.0)
            acc = acc + _dot(vt_ref[0, 0, kt], a_w.astype(BF16))
            return run_row + jnp.sum(lm, axis=0, keepdims=True), acc

        def alive(c):
            return (c[0] >= 0) & (jnp.max(c[1]) > SB_DEAD)

        def step(c):
            r, a = tile(c[0], c[1], c[2], False)
            return c[0] - 1, r, a

        def emit(acc):
            if nh == 2:
                o_t = jnp.where(row < HEAD_DIM, acc[:, 0:tq], acc[:, tq:2 * tq])
            else:
                o_t = jnp.where(row < HEAD_DIM, acc, 0.0)
            o_ref[0] = o_t.T.astype(BF16)

        zeros = (jnp.zeros((1, nh * tq), F32), jnp.zeros((LANES, nh * tq), F32))

        @pl.when(qi == 0)
        def _():
            emit(tile(qi, *zeros, True)[1])

        @pl.when(qi > 0)
        def _():
            run_row, acc = tile(qi, *zeros, True)
            run_row, acc = tile(qi - 1, run_row, acc, False)
            emit(lax.while_loop(alive, step, (qi - 2, run_row, acc))[2])

    @pl.when(pair * 2 + 1 < n_heads)
    def _():
        run(2)

    @pl.when(pair * 2 + 1 >= n_heads)
    def _():
        run(1)


def _sb_attention(sb, vt):
    b, s, _ = sb.shape
    tq = KEY_TILE
    kern = functools.partial(_sb_kernel, tq=tq, n_heads=SB_HEADS)
    return pl.pallas_call(
        kern,
        grid=(b, PAIR_SLABS, s // tq),
        in_specs=[pl.BlockSpec((1, tq, LANES), lambda i, p, j: (i, j, p)),
                  pl.BlockSpec((1, s, LANES), lambda i, p, j: (i, 0, PAIR_SLABS + p)),
                  pl.BlockSpec((1, 1, s // KEY_TILE, LANES, KEY_TILE), lambda i, p, j: (i, 2 + p, 0, 0, 0))],
        out_specs=pl.BlockSpec((1, tq, LANES), lambda i, p, j: (i, j, p)),
        out_shape=jax.ShapeDtypeStruct((b, s, PAIR_D), BF16),
        compiler_params=_cparams(("arbitrary", "arbitrary", "arbitrary")),
        name="sb_attention",
    )(sb, sb, vt)


def _mla_kernel(qn_ref, qr_ref, kn_ref, kpe_ref, vt_ref, o_ref, s_buf, *, tq, n_heads):
    pair = pl.program_id(1)
    t0 = pl.program_id(2) * tq
    per_sweep = SWEEP // KEY_TILE
    q = jnp.concatenate([qn_ref[0], qr_ref[0]], axis=1)
    lane2 = _iota((1, 2 * LANES), 1)
    lane_head = jnp.where(lane2 < LANES, lane2 // MLA_NOPE, (lane2 - LANES) // MLA_ROPE)
    t_row1 = t0 + _iota((1, tq), 1)
    row = _iota((LANES, 1), 0)
    scale = (MLA_NOPE + MLA_ROPE) ** -0.5
    ones_rows = jnp.ones((ONES_ROWS, KEY_TILE), BF16)
    n_full = t0 // SWEEP
    n_it = (t0 + tq + SWEEP - 1) // SWEEP

    def run(nh):
        m = nh * tq
        qs = jnp.concatenate([jnp.where(lane_head == a, q, jnp.zeros((), BF16))
                              for a in range(nh)], axis=0)
        t_row = _lanes([t_row1] * nh)

        def scores(it, m8, diagonal):
            k0 = pl.multiple_of(it * SWEEP, SWEEP)
            k = jnp.concatenate([kn_ref[0, pl.ds(k0, SWEEP), :], kpe_ref[0, pl.ds(k0, SWEEP), :]], axis=1)
            sc = _dot_nt(k, qs) * (scale * LOG2E)
            if diagonal:
                sc = jnp.where((k0 + _iota((SWEEP, 1), 0)) <= t_row, sc, -jnp.inf)
            s_buf[it, :, 0:m] = sc
            return jnp.maximum(m8, jnp.max(sc.reshape(-1, 8, m), axis=0))

        m8 = _sweep_loop(0, n_full, lambda it, c: scores(it, c, False), jnp.full((8, m), M_FLOOR, F32))
        m8 = lax.fori_loop(n_full, n_it, lambda it, c: scores(it, c, True), m8)
        m_f = jnp.max(m8, axis=0, keepdims=True)

        def values(it, acc):
            pb = jnp.exp2(s_buf[it, :, 0:m] - m_f).astype(BF16)
            for h in range(per_sweep):
                v_aug = jnp.concatenate([vt_ref[0, 0, it * per_sweep + h], ones_rows], axis=0)
                acc = acc + _dot(v_aug, pb[h * KEY_TILE:(h + 1) * KEY_TILE])
            return acc

        acc = _sweep_loop(0, n_it, values, jnp.zeros((LANES + ONES_ROWS, m), F32))
        o = acc[0:LANES] / acc[LANES:LANES + 1]
        if nh == 2:
            o_t = jnp.where(row < MLA_V, o[:, 0:tq], o[:, tq:2 * tq])
        else:
            o_t = jnp.where(row < MLA_V, o, 0.0)
        o_ref[0] = o_t.T.astype(BF16)

    @pl.when(pair * 2 + 1 < n_heads)
    def _():
        run(2)

    @pl.when(pair * 2 + 1 >= n_heads)
    def _():
        run(1)


def _mla_attention(mla, vt, tq=SWEEP):
    b, s, _ = mla.shape
    kern = functools.partial(_mla_kernel, tq=tq, n_heads=MLA_HEADS)
    ps = PAIR_SLABS
    return pl.pallas_call(
        kern,
        grid=(b, ps, s // tq),
        in_specs=[pl.BlockSpec((1, tq, LANES), lambda i, p, j: (i, j, p)),
                  pl.BlockSpec((1, tq, LANES), lambda i, p, j: (i, j, ps + p)),
                  pl.BlockSpec((1, s, LANES), lambda i, p, j: (i, 0, 2 * ps + p)),
                  pl.BlockSpec((1, s, LANES), lambda i, p, j: (i, 0, 3 * ps)),
                  pl.BlockSpec((1, 1, s // KEY_TILE, LANES, KEY_TILE),
                               lambda i, p, j: (i, 2 + ps + p, 0, 0, 0))],
        out_specs=pl.BlockSpec((1, tq, LANES), lambda i, p, j: (i, j, p)),
        out_shape=jax.ShapeDtypeStruct((b, s, PAIR_D), BF16),
        scratch_shapes=[pltpu.VMEM((s // SWEEP, SWEEP, 2 * tq), F32)],
        compiler_params=_cparams(("arbitrary", "arbitrary", "arbitrary")),
        name="mla_attention",
    )(mla, mla, mla, mla, vt)


def _outproj_kernel(oa_ref, ob_ref, oc_ref, z_ref, x_ref, mod_ref, w_ref, fg_ref, o_ref, *, final):
    mix = (jnp.concatenate([oa_ref[0], ob_ref[0], oc_ref[0]], axis=1).astype(F32)
           * jax.nn.silu(z_ref[0].astype(F32)))
    y = _dot(mix.astype(BF16), w_ref[...])
    xn = x_ref[0] + mod_ref[0, 2:3, :] * y
    if final:
        ms = jnp.mean(xn * xn, axis=-1, keepdims=True)
        xn = xn * lax.rsqrt(ms + EPS) * fg_ref[...]
    o_ref[0] = xn


def _out_projection(oa, ob, oc, z, x, mod3, w_out_p, final_g, final, ts=512):
    b, s, d = x.shape
    row = lambda n: pl.BlockSpec((1, ts, n), lambda i, j: (i, j, 0))
    full = lambda a: pl.BlockSpec(a.shape, lambda i, j: (0,) * a.ndim)
    return pl.pallas_call(
        functools.partial(_outproj_kernel, final=final),
        grid=(b, s // ts),
        in_specs=[row(NSA_D), row(PAIR_D), row(PAIR_D), row(3 * PAIR_D), row(d),
                  pl.BlockSpec((1, 3, d), lambda i, j: (i, 0, 0)), full(w_out_p), full(final_g)],
        out_specs=row(d),
        out_shape=jax.ShapeDtypeStruct((b, s, d), F32),
        compiler_params=_cparams(("arbitrary", "arbitrary")),
        name="out_projection",
    )(oa, ob, oc, z, x, mod3, w_out_p, final_g)


def _overlap_matrix_t(s):
    n_cmp = (s - CMP_LEN) // CMP_STRIDE + 1
    n_sel = s // SEL_BLOCK
    c_start = np.arange(s // CMP_STRIDE) * CMP_STRIDE
    s_start = np.arange(n_sel) * SEL_BLOCK
    ov = ((c_start[None, :] < s_start[:, None] + SEL_BLOCK)
          & (c_start[None, :] + CMP_LEN > s_start[:, None])
          & (np.arange(s // CMP_STRIDE)[None, :] < n_cmp))
    return jnp.asarray(ov.astype(np.float32), dtype=BF16)


def _mixer_layer(x, mod3, tabs, overlap_t, norm_g, w_in, pos_k, pos_v, ck_w1, ck_w2, cv_w1, cv_w2,
                 q_norm, w_uq, kv_norm, w_ukv, w_out, final_g, final):
    b, s, d = x.shape
    w_in_p = _take_cols(w_in, _in_cols()).astype(BF16)
    w_uq_p = jnp.pad(_take_cols(w_uq, _uq_cols()), ((0, Q_RANK_PAD - MLA_Q_RANK), (0, 0))).astype(BF16)
    w_ukv_p = _take_cols(w_ukv, _ukv_cols()).astype(BF16)
    q_norm_p = jnp.pad(q_norm, (0, Q_RANK_PAD - MLA_Q_RANK)).reshape(1, Q_RANK_PAD)
    w_out_p = _take_cols(w_out, _out_rows(), axis=0).astype(BF16)

    nq, nk, cmp_tok, gate_t, z, sb, mla, vt = _in_projection(
        x, mod3, norm_g.reshape(1, d), w_in_p, q_norm_p, w_uq_p, kv_norm.reshape(1, -1), w_ukv_p, tabs)

    pos4 = jnp.concatenate([_cmp_pos(pos_k), _cmp_pos(pos_v)], axis=0)
    kc, vct = _nsa_compress(cmp_tok, pos4, _cmp_weights(ck_w1, ck_w2), _cmp_weights(cv_w1, cv_w2))

    o_a = _nsa_attention(nq, gate_t, kc, vct, nk, vt, overlap_t)
    o_b = _sb_attention(sb, vt)
    o_c = _mla_attention(mla, vt)
    return _out_projection(o_a, o_b, o_c, z, x, mod3, w_out_p, final_g.reshape(1, d), final)


def kernel(x, c, positions, ada_w, ada_b, norm_g, w_in, nsa_pos_k, nsa_pos_v, nsa_ck_w1, nsa_ck_w2,
           nsa_cv_w1, nsa_cv_w2, mla_q_norm, mla_w_uq, mla_kv_norm, mla_w_ukv, w_out, final_norm):
    b, s, d = x.shape
    depth = w_in.shape[0]
    assert s % SWEEP == 0 and s >= WINDOW + 2 * KEY_TILE
    mod = _adaln_mod(c, ada_w, ada_b).reshape(depth, b, 3, d)
    tabs = _rope_tables(positions)
    overlap_t = _overlap_matrix_t(s)
    for l in range(depth):
        x = _mixer_layer(x, mod[l], tabs, overlap_t, norm_g[l], w_in[l], nsa_pos_k[l], nsa_pos_v[l],
                         nsa_ck_w1[l], nsa_ck_w2[l], nsa_cv_w1[l], nsa_cv_w2[l],
                         mla_q_norm[l], mla_w_uq[l], mla_kv_norm[l], mla_w_ukv[l], w_out[l],
                         final_norm, final=(l == depth - 1))
    return x
```
